```python
import math
import jax
import jax.numpy as jnp
from jax import lax
import numpy as np

D_MODEL = 4096
BATCH = 1
SEQ = 8192
DEPTH = 2

GRID_W = 64
CTX_LEN = 256
SSD_HEAD_DIM = 64
SSD_WIDTH = D_MODEL
SSD_HEADS = SSD_WIDTH // SSD_HEAD_DIM
SSD_GROUPS = 8
SSD_STATE = 128
SSD_CHUNK = 128
SSD_CONV = 5
SSD_GN = SSD_GROUPS * SSD_STATE
SSD_CONV_CH = SSD_WIDTH + 2 * SSD_GN
CF_WIDTH = D_MODEL // 2
CF_KERNEL = 31
FN_WIDTH = D_MODEL // 2
FN_GROUPS = 8
FN_GROUP_DIM = FN_WIDTH // FN_GROUPS
N_BRANCHES = 3
W_MIX = SSD_WIDTH + CF_WIDTH + FN_WIDTH
IN_SIZES = (SSD_CONV_CH, SSD_WIDTH, 2 * SSD_HEADS, 2 * CF_WIDTH, CF_WIDTH, FN_WIDTH, FN_WIDTH, N_BRANCHES * D_MODEL)
IN_COLS = sum(IN_SIZES)
IN_SPLIT_POINTS = tuple(int(s) for s in np.cumsum(IN_SIZES)[:-1])
EPS = 1e-6
DT_MIN = 1e-3
DT_MAX = 1e-1
A_INIT_MAX = 16.0

kernel_name = 'hybrid_ssd_conformer_fourier_dit_block'


def rms_norm(x, g):
    xf = x.astype(jnp.float32)
    y = xf * lax.rsqrt(jnp.mean(xf * xf, axis=-1, keepdims=True) + EPS)
    return (y * g.astype(jnp.float32)).astype(x.dtype)


def layer_norm(x, g, b):
    xf = x.astype(jnp.float32)
    mu = jnp.mean(xf, axis=-1, keepdims=True)
    xc = xf - mu
    y = xc * lax.rsqrt(jnp.mean(xc * xc, axis=-1, keepdims=True) + EPS)
    return (y * g.astype(jnp.float32) + b.astype(jnp.float32)).astype(x.dtype)


def depthwise_conv(x, w, b):
    k, ch = w.shape
    out = lax.conv_general_dilated(x, w[:, None, :].astype(x.dtype), window_strides=(1,),
                                   padding=[(k // 2, k // 2)],
                                   dimension_numbers=('NWC', 'WIO', 'NWC'),
                                   feature_group_count=ch)
    return out + b.astype(x.dtype)


def ssd_chunked(xs, dt, a_log, bm, cm, h0):
    bsz, seq = xs.shape[:2]
    q, g, n, p = SSD_CHUNK, SSD_GROUPS, SSD_STATE, SSD_HEAD_DIM
    r = SSD_HEADS // g
    nc = seq // q
    a_neg = -jnp.exp(a_log.astype(jnp.float32))
    x = xs.astype(jnp.float32).reshape(bsz, nc, q, g, r, p)
    dtc = dt.reshape(bsz, nc, q, g, r)
    a = dtc * a_neg.reshape(g, r)
    xdt = x * dtc[..., None]
    bq = bm.astype(jnp.float32).reshape(bsz, nc, q, g, n)
    cq = cm.astype(jnp.float32).reshape(bsz, nc, q, g, n)
    a_cum = jnp.cumsum(a, axis=2)
    seg = a_cum[:, :, :, None] - a_cum[:, :, None, :]
    lower = jnp.tril(jnp.ones((q, q), dtype=bool))[:, :, None, None]
    decay = jnp.exp(jnp.where(lower, seg, -jnp.inf))
    scores = jnp.einsum('bcign,bcjgn->bcijg', cq, bq)
    y_diag = jnp.einsum('bcijg,bcijgr,bcjgrp->bcigrp', scores, decay, xdt)
    decay_to_end = jnp.exp(a_cum[:, :, -1:] - a_cum)
    states = jnp.einsum('bcjgn,bcjgr,bcjgrp->bcgrpn', bq, decay_to_end, xdt)
    chunk_decay = jnp.exp(a_cum[:, :, -1])

    def step(h, inp):
        s, d = inp
        return h * d[..., None, None] + s, h

    h_final, h_in = lax.scan(step, h0.reshape(bsz, g, r, p, n),
                             (jnp.moveaxis(states, 1, 0), jnp.moveaxis(chunk_decay, 1, 0)))
    h_in = jnp.moveaxis(h_in, 0, 1)
    y_off = jnp.einsum('bcign,bcgrpn,bcigr->bcigrp', cq, h_in, jnp.exp(a_cum))
    y = (y_diag + y_off).reshape(bsz, seq, SSD_HEADS, p)
    return y, h_final.reshape(bsz, SSD_HEADS, p, n)


def ssd_stream(xbc, dt_raw, h0_f, h0_b, conv_w, conv_b, dt_bias, a_log, d_skip):
    u = jax.nn.silu(depthwise_conv(xbc, conv_w, conv_b))
    bsz, seq, _ = u.shape
    xs, bm, cm = jnp.split(u, [SSD_WIDTH, SSD_WIDTH + SSD_GN], axis=-1)
    xs = xs.reshape(bsz, seq, SSD_HEADS, SSD_HEAD_DIM)
    bm = bm.reshape(bsz, seq, SSD_GROUPS, SSD_STATE)
    cm = cm.reshape(bsz, seq, SSD_GROUPS, SSD_STATE)
    dt = jax.nn.softplus(dt_raw.astype(jnp.float32).reshape(bsz, seq, 2, SSD_HEADS)
                         + dt_bias.astype(jnp.float32))
    flip = lambda t: jnp.flip(t, axis=1)
    y_f, h_f = ssd_chunked(xs, dt[:, :, 0], a_log[0], bm, cm, h0_f)
    y_b, h_b = ssd_chunked(flip(xs), flip(dt[:, :, 1]), a_log[1], flip(bm), flip(cm), h0_b)
    y = y_f + flip(y_b) + d_skip.astype(jnp.float32)[:, None] * xs.astype(jnp.float32)
    return y.reshape(bsz, seq, SSD_WIDTH).astype(xbc.dtype), h_f, h_b


def conformer_conv_branch(glu_in, z, conv_w, conv_b, ln_g, ln_b, on_grid):
    a, b = jnp.split(glu_in, 2, axis=-1)
    u = a * jax.nn.sigmoid(b)
    bsz, seq, ch = u.shape
    if on_grid:
        rows = seq // GRID_W
        v = depthwise_conv(u.reshape(bsz * rows, GRID_W, ch), conv_w, conv_b).reshape(bsz, seq, ch)
    else:
        v = depthwise_conv(u, conv_w, conv_b)
    v = layer_norm(v, ln_g, ln_b)
    return jax.nn.silu(v) * jax.nn.silu(z)


def fourier_branch(v, z):
    bsz, seq, _ = v.shape
    vg = v.astype(jnp.float32).reshape(bsz, seq, FN_GROUPS, FN_GROUP_DIM)
    f = jnp.fft.fft2(vg, axes=(1, 3), norm='ortho').real.reshape(bsz, seq, FN_WIDTH)
    return f.astype(v.dtype) * jax.nn.silu(z)


def merge_branches(y_a, y_b, y_c, gate_pre, w_branch, w_out):
    p_a = y_a @ w_branch[:SSD_WIDTH]
    p_b = y_b @ w_branch[SSD_WIDTH:SSD_WIDTH + CF_WIDTH]
    p_c = y_c @ w_branch[SSD_WIDTH + CF_WIDTH:]
    g_a, g_b, g_c = jnp.split(jax.nn.sigmoid(gate_pre), 3, axis=-1)
    return (g_a * p_a + g_b * p_b + g_c * p_c) @ w_out


def hybrid_layer(x, ctx, c, c_ctx, w_mod, b_mod, norm_g, w_in, ssd_conv_w, ssd_conv_b,
                 ssd_dt_bias, ssd_a_log, ssd_d, ssd_norm_g, cf_conv_w, cf_conv_b, cf_ln_g,
                 cf_ln_b, w_branch, w_out, update_ctx):
    shift_x, scale_x, gate_x = jnp.split((jax.nn.silu(c) @ w_mod + b_mod)[:, None, :], 3, axis=-1)
    shift_c, scale_c, gate_c = jnp.split(jax.nn.silu(c_ctx) @ w_mod + b_mod, 3, axis=-1)
    h_x = rms_norm(x, norm_g) * (1 + scale_x) + shift_x
    h_c = rms_norm(ctx, norm_g) * (1 + scale_c) + shift_c
    xbc_x, za_x, dt_x, glu_x, zb_x, v_x, zc_x, gates_x = jnp.split(h_x @ w_in, IN_SPLIT_POINTS, axis=-1)
    xbc_c, za_c, dt_c, glu_c, zb_c, v_c, zc_c, gates_c = jnp.split(h_c @ w_in, IN_SPLIT_POINTS, axis=-1)
    ssd_params = (ssd_conv_w, ssd_conv_b, ssd_dt_bias, ssd_a_log, ssd_d)
    h_zero = jnp.zeros((x.shape[0], SSD_HEADS, SSD_HEAD_DIM, SSD_STATE), jnp.float32)
    ya_c, h_f, h_b = ssd_stream(xbc_c, dt_c, h_zero, h_zero, *ssd_params)
    ya_x, _, _ = ssd_stream(xbc_x, dt_x, h_f, h_b, *ssd_params)
    ya_x = rms_norm(ya_x * jax.nn.silu(za_x), ssd_norm_g)
    yb_x = conformer_conv_branch(glu_x, zb_x, cf_conv_w, cf_conv_b, cf_ln_g, cf_ln_b, True)
    yc_x = fourier_branch(v_x, zc_x)
    x_new = x + gate_x * merge_branches(ya_x, yb_x, yc_x, gates_x, w_branch, w_out)
    if update_ctx:
        ya_c = rms_norm(ya_c * jax.nn.silu(za_c), ssd_norm_g)
        yb_c = conformer_conv_branch(glu_c, zb_c, cf_conv_w, cf_conv_b, cf_ln_g, cf_ln_b, False)
        yc_c = fourier_branch(v_c, zc_c)
        ctx_new = ctx + gate_c * merge_branches(ya_c, yb_c, yc_c, gates_c, w_branch, w_out)
    else:
        ctx_new = ctx
    return x_new, ctx_new


def setup_inputs(seed: int = 0) -> dict:
    key = jax.random.key(seed)
    ks = jax.random.split(key, 24)
    nrm = jax.random.normal
    d = D_MODEL
    x = nrm(ks[0], (BATCH, SEQ, d), jnp.float32)
    c = nrm(ks[1], (BATCH, d), jnp.float32)
    ctx = nrm(ks[2], (BATCH, CTX_LEN, d), jnp.float32)
    c_ctx = nrm(ks[3], (d,), jnp.float32)
    w_mod = nrm(ks[4], (DEPTH, d, 3 * d), jnp.float32) * (0.5 * d ** -0.5)
    b_mod = 0.01 * nrm(ks[5], (DEPTH, 3 * d), jnp.float32)
    norm_g = 1.0 + 0.02 * nrm(ks[6], (DEPTH, d), jnp.float32)
    w_in = nrm(ks[7], (DEPTH, d, IN_COLS), jnp.float32) * d ** -0.5
    ssd_conv_w = nrm(ks[8], (DEPTH, SSD_CONV, SSD_CONV_CH), jnp.float32) * SSD_CONV ** -0.5
    ssd_conv_b = 0.02 * nrm(ks[9], (DEPTH, SSD_CONV_CH), jnp.float32)
    u = jax.random.uniform(ks[10], (DEPTH, 2, SSD_HEADS), jnp.float32)
    dt0 = jnp.exp(u * (math.log(DT_MAX) - math.log(DT_MIN)) + math.log(DT_MIN))
    ssd_dt_bias = dt0 + jnp.log(-jnp.expm1(-dt0))
    ssd_a_log = jnp.log(jax.random.uniform(ks[11], (DEPTH, 2, SSD_HEADS), jnp.float32, 1.0, A_INIT_MAX))
    ssd_d = 1.0 + 0.1 * nrm(ks[12], (DEPTH, SSD_HEADS), jnp.float32)
    ssd_norm_g = 1.0 + 0.02 * nrm(ks[13], (DEPTH, SSD_WIDTH), jnp.float32)
    cf_conv_w = nrm(ks[14], (DEPTH, CF_KERNEL, CF_WIDTH), jnp.float32) * CF_KERNEL ** -0.5
    cf_conv_b = 0.02 * nrm(ks[15], (DEPTH, CF_WIDTH), jnp.float32)
    cf_ln_g = 1.0 + 0.02 * nrm(ks[16], (DEPTH, CF_WIDTH), jnp.float32)
    cf_ln_b = 0.02 * nrm(ks[17], (DEPTH, CF_WIDTH), jnp.float32)
    w_branch = jnp.concatenate([
        nrm(ks[18], (DEPTH, SSD_WIDTH, d), jnp.float32) * SSD_WIDTH ** -0.5,
        nrm(ks[19], (DEPTH, CF_WIDTH, d), jnp.float32) * CF_WIDTH ** -0.5,
        nrm(ks[20], (DEPTH, FN_WIDTH, d), jnp.float32) * FN_WIDTH ** -0.5], axis=1)
    w_out = nrm(ks[21], (DEPTH, d, d), jnp.float32) * d ** -0.5
    final_g = 1.0 + 0.02 * nrm(ks[22], (d,), jnp.float32)
    return {'x': x, 'c': c, 'ctx': ctx, 'c_ctx': c_ctx, 'w_mod': w_mod, 'b_mod': b_mod,
            'norm_g': norm_g, 'w_in': w_in, 'ssd_conv_w': ssd_conv_w, 'ssd_conv_b': ssd_conv_b,
            'ssd_dt_bias': ssd_dt_bias, 'ssd_a_log': ssd_a_log, 'ssd_d': ssd_d,
            'ssd_norm_g': ssd_norm_g, 'cf_conv_w': cf_conv_w, 'cf_conv_b': cf_conv_b,
            'cf_ln_g': cf_ln_g, 'cf_ln_b': cf_ln_b, 'w_branch': w_branch, 'w_out': w_out,
            'final_g': final_g}


def reference(x, c, ctx, c_ctx, w_mod, b_mod, norm_g, w_in, ssd_conv_w, ssd_conv_b, ssd_dt_bias,
              ssd_a_log, ssd_d, ssd_norm_g, cf_conv_w, cf_conv_b, cf_ln_g, cf_ln_b, w_branch,
              w_out, final_g):
    for i in range(DEPTH):
        x, ctx = hybrid_layer(x, ctx, c, c_ctx, w_mod[i], b_mod[i], norm_g[i], w_in[i],
                              ssd_conv_w[i], ssd_conv_b[i], ssd_dt_bias[i], ssd_a_log[i], ssd_d[i],
                              ssd_norm_g[i], cf_conv_w[i], cf_conv_b[i], cf_ln_g[i], cf_ln_b[i],
                              w_branch[i], w_out[i], update_ctx=(i < DEPTH - 1))
    return rms_norm(x, final_g)
```

```python
import functools
import math

import numpy as np
import jax
import jax.numpy as jnp
from jax import lax
from jax.experimental import pallas as pl
from jax.experimental.pallas import tpu as pltpu

F32 = jnp.float32
BF16 = jnp.bfloat16

SSD_GROUPS = 8
SSD_STATE = 128
SSD_CHUNK = 128
GRID_W = 64
FN_GROUPS = 8
EPS = 1e-6
FFT_L1 = 64

SUBLANES = 8
LANES = 128
BF16_SUBLANES = 16
VMEM_LIMIT_BYTES = 56 * 1024 * 1024
HALO = 16


def _params(*sem):
    return pltpu.CompilerParams(dimension_semantics=sem, vmem_limit_bytes=VMEM_LIMIT_BYTES)


def _pick_tile(n, cap, mult):
    best = None
    t = mult
    while t <= min(n, cap):
        if n % t == 0:
            best = t
        t += mult
    assert best is not None, (n, cap, mult)
    return best


def _silu(v):
    return v * jax.nn.sigmoid(v)


def _mod_kernel(s_ref, w_ref, b_ref, o_ref):
    s = _silu(s_ref[...])
    o_ref[...] = jnp.dot(s.astype(BF16), w_ref[...].astype(BF16),
                         preferred_element_type=F32) + b_ref[...]


def _modulation(cond8, w_mod, b_mod):
    depth, d, n3 = w_mod.shape
    tn = _pick_tile(n3, 512, LANES)
    return pl.pallas_call(
        _mod_kernel,
        out_shape=jax.ShapeDtypeStruct((depth, SUBLANES, n3), F32),
        grid=(depth, n3 // tn),
        in_specs=[pl.BlockSpec((SUBLANES, d), lambda l, j: (0, 0)),
                  pl.BlockSpec((None, d, tn), lambda l, j: (l, 0, j)),
                  pl.BlockSpec((None, 1, tn), lambda l, j: (l, 0, j))],
        out_specs=pl.BlockSpec((None, SUBLANES, tn), lambda l, j: (l, 0, j)),
        compiler_params=_params("parallel", "parallel"),
        name="modulation",
    )(cond8, w_mod, b_mod.reshape(depth, 1, n3))


def _norm_mod_kernel(x_ref, g_ref, sh_ref, sc_ref, o_ref, *, row):
    x = x_ref[...]
    y = x * lax.rsqrt(jnp.mean(x * x, axis=-1, keepdims=True) + EPS) * g_ref[...]
    sc = sc_ref[row:row + 1, :]
    sh = sh_ref[row:row + 1, :]
    o_ref[...] = (y * (1.0 + sc) + sh).astype(o_ref.dtype)


def _norm_mod(x2, g, mod_all, layer, row):
    m, d = x2.shape
    tm = _pick_tile(m, 256, SUBLANES)
    return pl.pallas_call(
        functools.partial(_norm_mod_kernel, row=row),
        out_shape=jax.ShapeDtypeStruct((m, d), BF16),
        grid=(m // tm,),
        in_specs=[pl.BlockSpec((tm, d), lambda i: (i, 0)),
                  pl.BlockSpec((1, d), lambda i: (0, 0)),
                  pl.BlockSpec((None, SUBLANES, d), lambda i: (layer, 0, 0)),
                  pl.BlockSpec((None, SUBLANES, d), lambda i: (layer, 0, 1))],
        out_specs=pl.BlockSpec((tm, d), lambda i: (i, 0)),
        compiler_params=_params("parallel"),
        name="norm_mod",
    )(x2, g.reshape(1, d), mod_all, mod_all)


def _final_norm_kernel(x_ref, g_ref, o_ref):
    x = x_ref[...]
    o_ref[...] = x * lax.rsqrt(jnp.mean(x * x, axis=-1, keepdims=True) + EPS) * g_ref[...]


def _final_norm(x2, g):
    m, d = x2.shape
    tm = _pick_tile(m, 256, SUBLANES)
    return pl.pallas_call(
        _final_norm_kernel,
        out_shape=jax.ShapeDtypeStruct((m, d), F32),
        grid=(m // tm,),
        in_specs=[pl.BlockSpec((tm, d), lambda i: (i, 0)),
                  pl.BlockSpec((1, d), lambda i: (0, 0))],
        out_specs=pl.BlockSpec((tm, d), lambda i: (i, 0)),
        compiler_params=_params("parallel"),
        name="final_norm",
    )(x2, g.reshape(1, d))


def _mm_kernel(a_ref, b_ref, o_ref):
    o_ref[...] = jnp.dot(a_ref[...], b_ref[...], preferred_element_type=F32).astype(o_ref.dtype)


def _matmul(a, b, out_dtype, *, rows, cols, col_off, tm_cap, tn_cap, name):
    k = a.shape[1]
    tm = _pick_tile(rows, tm_cap, LANES)
    tn = _pick_tile(math.gcd(cols, col_off) if col_off else cols, tn_cap, min(LANES, cols))
    joff = col_off // tn
    return pl.pallas_call(
        _mm_kernel,
        out_shape=jax.ShapeDtypeStruct((rows, cols), out_dtype),
        grid=(rows // tm, cols // tn),
        in_specs=[pl.BlockSpec((tm, k), lambda i, j: (i, 0)),
                  pl.BlockSpec((k, tn), lambda i, j: (0, j + joff))],
        out_specs=pl.BlockSpec((tm, tn), lambda i, j: (i, j)),
        compiler_params=_params("parallel", "parallel"),
        name=name,
    )(a, b)


def _mm_res_kernel(a_ref, b_ref, r_ref, g_ref, o_ref, *, row):
    acc = jnp.dot(a_ref[...], b_ref[...], preferred_element_type=F32)
    o_ref[...] = r_ref[...] + g_ref[row:row + 1, :] * acc


def _out_proj(merged, w_out, resid, mod_all, layer, row):
    m, d = resid.shape
    tm = _pick_tile(m, 1024, LANES)
    tn = _pick_tile(d, 512, LANES)
    goff = 2 * d // tn
    return pl.pallas_call(
        functools.partial(_mm_res_kernel, row=row),
        out_shape=jax.ShapeDtypeStruct((m, d), F32),
        grid=(m // tm, d // tn),
        in_specs=[pl.BlockSpec((tm, d), lambda i, j: (i, 0)),
                  pl.BlockSpec((d, tn), lambda i, j: (0, j)),
                  pl.BlockSpec((tm, tn), lambda i, j: (i, j)),
                  pl.BlockSpec((None, SUBLANES, tn), lambda i, j: (layer, 0, j + goff))],
        out_specs=pl.BlockSpec((tm, tn), lambda i, j: (i, j)),
        compiler_params=_params("parallel", "parallel"),
        name="out_proj",
    )(merged, w_out, resid, mod_all)


def _split3(a):
    hi = a.astype(BF16)
    r1 = a - hi.astype(F32)
    mid = r1.astype(BF16)
    lo = (r1 - mid.astype(F32)).astype(BF16)
    return hi, mid, lo


def _ssd_pre_kernel(cur_ref, prev_ref, next_ref, dtraw_ref, cw_ref, cb_ref, dtb_ref, alog_ref,
                    xs_ref, bt_ref, c_ref, cum_ref, cumt_ref, dtt_ref, wt_ref, ext_ref,
                    *, nbx, nb, d, rp, heads):
    q = SSD_CHUNK
    n = SSD_STATE
    i = pl.program_id(0)
    first = jnp.logical_or(i == 0, i == nbx)
    last = jnp.logical_or(i == nbx - 1, i == nb - 1)
    ext_ref[HALO:HALO + q, :] = cur_ref[...].astype(F32)
    ext_ref[0:HALO, :] = prev_ref[...].astype(F32) * jnp.where(first, 0.0, 1.0)
    ext_ref[HALO + q:2 * HALO + q, :] = next_ref[...].astype(F32) * jnp.where(last, 0.0, 1.0)
    taps = cw_ref.shape[0]
    half = taps // 2

    def conv_silu(c0, width):
        acc = jnp.broadcast_to(cb_ref[:, c0:c0 + width], (q, width))
        for k in range(taps):
            acc = acc + ext_ref[pl.ds(HALO - half + k, q), c0:c0 + width] * cw_ref[k:k + 1, c0:c0 + width]
        return _silu(acc)

    for g in range(SSD_GROUPS):
        xs_ref[g] = conv_silu(g * rp, rp).astype(xs_ref.dtype)
    for g in range(SSD_GROUPS):
        bt_ref[g] = conv_silu(d + g * n, n).T.astype(bt_ref.dtype)
    for g in range(SSD_GROUPS):
        c_ref[g] = conv_silu(d + SSD_GROUPS * n + g * n, n).astype(c_ref.dtype)

    raw = dtraw_ref[...] + dtb_ref[...]
    dt = jnp.maximum(raw, 0.0) + jnp.log1p(jnp.exp(-jnp.abs(raw)))
    a = dt * (-jnp.exp(alog_ref[...]))
    row = lax.broadcasted_iota(jnp.int32, (q, q), 0)
    col = lax.broadcasted_iota(jnp.int32, (q, q), 1)
    tri_f = jnp.where(col <= row, 1.0, 0.0).astype(BF16)
    tri_b = jnp.where(col >= row, 1.0, 0.0).astype(BF16)
    cf = jnp.zeros((q, 2 * heads), F32)
    cb = jnp.zeros((q, 2 * heads), F32)
    for piece in _split3(a):
        cf = cf + jnp.dot(tri_f, piece, preferred_element_type=F32)
        cb = cb + jnp.dot(tri_b, piece, preferred_element_type=F32)
    is_fwd = lax.broadcasted_iota(jnp.int32, (q, 2 * heads), 1) < heads
    cum = jnp.where(is_fwd, cf, cb)
    clast = jnp.where(is_fwd[0:1, :], cum[q - 1:q, :], cum[0:1, :])
    w = dt * jnp.exp(clast - cum)
    cum_ref[...] = cum
    cumt_ref[...] = cum.T
    dtt_ref[...] = dt.T
    wt_ref[...] = w.T


def _ssd_pre(proj, dt_raw, conv_w, conv_b, dt_bias, a_log, *, nx, d, heads):
    mt = proj.shape[0]
    q, n, g = SSD_CHUNK, SSD_STATE, SSD_GROUPS
    cc = d + 2 * g * n
    rp = d // g
    nb, nbx = mt // q, nx // q
    per = q // HALO
    nhb = mt // HALO
    h2 = 2 * heads
    kern = functools.partial(_ssd_pre_kernel, nbx=nbx, nb=nb, d=d, rp=rp, heads=heads)
    return pl.pallas_call(
        kern,
        out_shape=(jax.ShapeDtypeStruct((g, mt, rp), BF16),
                   jax.ShapeDtypeStruct((g, n, mt), BF16),
                   jax.ShapeDtypeStruct((g, mt, n), BF16),
                   jax.ShapeDtypeStruct((mt, h2), F32),
                   jax.ShapeDtypeStruct((nb, h2, q), F32),
                   jax.ShapeDtypeStruct((nb, h2, q), F32),
                   jax.ShapeDtypeStruct((nb, h2, q), F32)),
        grid=(nb,),
        in_specs=[pl.BlockSpec((q, cc), lambda i: (i, 0)),
                  pl.BlockSpec((HALO, cc), lambda i: (jnp.maximum(i * per - 1, 0), 0)),
                  pl.BlockSpec((HALO, cc), lambda i: (jnp.minimum((i + 1) * per, nhb - 1), 0)),
                  pl.BlockSpec((q, h2), lambda i: (i, 0)),
                  pl.BlockSpec(conv_w.shape, lambda i: (0, 0)),
                  pl.BlockSpec((1, cc), lambda i: (0, 0)),
                  pl.BlockSpec((1, h2), lambda i: (0, 0)),
                  pl.BlockSpec((1, h2), lambda i: (0, 0))],
        out_specs=(pl.BlockSpec((g, q, rp), lambda i: (0, i, 0)),
                   pl.BlockSpec((g, n, q), lambda i: (0, 0, i)),
                   pl.BlockSpec((g, q, n), lambda i: (0, i, 0)),
                   pl.BlockSpec((q, h2), lambda i: (i, 0)),
                   pl.BlockSpec((None, h2, q), lambda i: (i, 0, 0)),
                   pl.BlockSpec((None, h2, q), lambda i: (i, 0, 0)),
                   pl.BlockSpec((None, h2, q), lambda i: (i, 0, 0))),
        scratch_shapes=[pltpu.VMEM((q + 2 * HALO, cc), F32)],
        compiler_params=_params("parallel"),
        name="ssd_pre",
    )(proj, proj, proj, dt_raw, conv_w, conv_b.reshape(1, cc), dt_bias.reshape(1, h2),
      a_log.reshape(1, h2))


def _ssd_sweep_kernel(order_ref, xs_ref, bt_ref, c_ref, cum_ref, cumt_ref, dtt_ref, wt_ref, *rest,
                      rev, heads, hpg, d):
    del order_ref
    q, n, g_cnt = SSD_CHUNK, SSD_STATE, SSD_GROUPS
    if rev:
        yf_ref, za_ref, dexp_ref, ng_ref, out_ref, h_ref, yall_ref = rest
    else:
        out_ref, h_ref = rest
    pw = LANES
    p = pw // 2
    rp = hpg * p
    hoff = heads if rev else 0

    @pl.when(pl.program_id(0) == 0)
    def _():
        h_ref[...] = jnp.zeros_like(h_ref)

    cum = cum_ref[...]
    lane_h = lax.broadcasted_iota(jnp.int32, cum.shape, 1)
    row = lax.broadcasted_iota(jnp.int32, (q, q), 0)
    col = lax.broadcasted_iota(jnp.int32, (q, q), 1)
    tri = (col >= row) if rev else (col <= row)
    left = lax.broadcasted_iota(jnp.int32, (q, pw), 1) < p
    left1 = left[0:1, :]
    edge = 0 if rev else q - 1

    def group_body(g, carry):
        xs_g = xs_ref[g]
        c_g = c_ref[g]
        bt_g = bt_ref[g]
        s_mat = jnp.dot(c_g, bt_g, preferred_element_type=F32)
        c_f = c_g.astype(F32)
        bt_f = bt_g.astype(F32)
        for pair in range(hpg // 2):
            sl = slice(pair * pw, (pair + 1) * pw)
            xs2 = xs_g[:, sl]
            zero_x = jnp.zeros_like(xs2)
            xs_l = jnp.where(left, xs2, zero_x)
            xs_r = jnp.where(left, zero_x, xs2)
            h2 = h_ref[g, :, sl]
            h2b = h2.astype(BF16)
            zero_h = jnp.zeros_like(h2b)
            hs_l = jnp.where(left, h2b, zero_h)
            hs_r = jnp.where(left, zero_h, h2b)
            w_l, cs_l, bts_l, cd_l = [], [], [], []
            for e in range(2):
                hidx = hoff + g * hpg + 2 * pair + e
                colc = jnp.sum(jnp.where(lane_h == hidx, cum, 0.0), axis=1, keepdims=True)
                colb = jnp.broadcast_to(colc, (q, q))
                rowc = cumt_ref[pl.ds(hidx, 1), :]
                rowdt = dtt_ref[pl.ds(hidx, 1), :]
                roww = wt_ref[pl.ds(hidx, 1), :]
                dec = jnp.exp(jnp.where(tri, colb - rowc, -1e30))
                w_l.append((s_mat * dec * rowdt).astype(BF16))
                ecol = jnp.exp(colb)
                cs_l.append((c_f * ecol).astype(BF16))
                bts_l.append((bt_f * roww).astype(BF16))
                cd_l.append(ecol[edge:edge + 1, :])
            lhs_y = jnp.concatenate(w_l + cs_l, axis=1)
            rhs_y = jnp.concatenate([xs_l, xs_r, hs_l, hs_r], axis=0)
            y2 = jnp.dot(lhs_y, rhs_y, preferred_element_type=F32)
            st2 = jnp.dot(jnp.concatenate(bts_l, axis=1), jnp.concatenate([xs_l, xs_r], axis=0),
                          preferred_element_type=F32)
            cd2 = jnp.where(left1, cd_l[0], cd_l[1])
            h_ref[g, :, sl] = h2 * cd2 + st2
            if rev:
                yall_ref[g, :, sl] = y2 + yf_ref[g, :, sl]
            else:
                out_ref[g, :, sl] = y2
        return carry

    lax.fori_loop(0, g_cnt, group_body, 0)

    if rev:
        ssq = jnp.zeros((q, 1), F32)
        for g in range(g_cnt):
            gs = slice(g * rp, (g + 1) * rp)
            y = yall_ref[g] + dexp_ref[:, gs] * xs_ref[g].astype(F32)
            t = y * _silu(za_ref[:, gs].astype(F32))
            yall_ref[g] = t
            ssq = ssq + jnp.sum(t * t, axis=1, keepdims=True)
        scale = lax.rsqrt(ssq / d + EPS)
        for g in range(g_cnt):
            gs = slice(g * rp, (g + 1) * rp)
            out_ref[:, gs] = (yall_ref[g] * scale * ng_ref[:, gs]).astype(out_ref.dtype)


def _ssd_sweep(order, xs, bt, cm, cum, cumt, dtt, wt, *, rev, heads, d, extra=None):
    g, mt, rp = xs.shape
    q, n = SSD_CHUNK, SSD_STATE
    nb = mt // q
    h2 = 2 * heads
    hpg = heads // g
    assert hpg % 2 == 0 and rp == hpg * (LANES // 2)
    in_specs = [pl.BlockSpec((g, q, rp), lambda s, o: (0, o[s], 0)),
                pl.BlockSpec((g, n, q), lambda s, o: (0, 0, o[s])),
                pl.BlockSpec((g, q, n), lambda s, o: (0, o[s], 0)),
                pl.BlockSpec((q, h2), lambda s, o: (o[s], 0)),
                pl.BlockSpec((None, h2, q), lambda s, o: (o[s], 0, 0)),
                pl.BlockSpec((None, h2, q), lambda s, o: (o[s], 0, 0)),
                pl.BlockSpec((None, h2, q), lambda s, o: (o[s], 0, 0))]
    args = [xs, bt, cm, cum, cumt, dtt, wt]
    scratch = [pltpu.VMEM((g, n, rp), F32)]
    if rev:
        yf, proj, za_blk, dexp, ng = extra
        in_specs += [pl.BlockSpec((g, q, rp), lambda s, o: (0, o[s], 0)),
                     pl.BlockSpec((q, d), lambda s, o: (o[s], za_blk)),
                     pl.BlockSpec((1, d), lambda s, o: (0, 0)),
                     pl.BlockSpec((1, d), lambda s, o: (0, 0))]
        args += [yf, proj, dexp, ng]
        out_shape = jax.ShapeDtypeStruct((mt, d), BF16)
        out_spec = pl.BlockSpec((q, d), lambda s, o: (o[s], 0))
        scratch.append(pltpu.VMEM((g, q, rp), F32))
    else:
        out_shape = jax.ShapeDtypeStruct((g, mt, rp), F32)
        out_spec = pl.BlockSpec((g, q, rp), lambda s, o: (0, o[s], 0))
    kern = functools.partial(_ssd_sweep_kernel, rev=rev, heads=heads, hpg=hpg, d=d)
    return pl.pallas_call(
        kern,
        out_shape=out_shape,
        grid_spec=pltpu.PrefetchScalarGridSpec(
            num_scalar_prefetch=1, grid=(nb,), in_specs=in_specs, out_specs=out_spec,
            scratch_shapes=scratch),
        compiler_params=_params("arbitrary"),
        name="ssd_sweep_bwd" if rev else "ssd_sweep_fwd",
    )(order, *args)


def _conformer_kernel(glu_ref, zb_ref, cw_ref, cb_ref, lg_ref, lb_ref, o_ref, pad_ref, v_ref,
                      *, seg, nseg, cf):
    taps = cw_ref.shape[0]
    half = taps // 2
    rc = min(seg, 64)
    ct = 2 * LANES
    u = glu_ref[:, 0:cf].astype(F32) * jax.nn.sigmoid(glu_ref[:, cf:2 * cf].astype(F32))
    zeros = jnp.zeros((HALO, cf), F32)
    for s in range(nseg):
        pad_ref[s, 0:HALO, :] = zeros
        pad_ref[s, HALO:HALO + seg, :] = u[s * seg:(s + 1) * seg, :]
        pad_ref[s, HALO + seg:2 * HALO + seg, :] = zeros
    for c0 in range(0, cf, ct):
        for s in range(nseg):
            for r0 in range(0, seg, rc):
                acc = jnp.broadcast_to(cb_ref[:, c0:c0 + ct], (rc, ct))
                for k in range(taps):
                    acc = acc + (pad_ref[s, pl.ds(HALO + r0 + k - half, rc), c0:c0 + ct]
                                 * cw_ref[k:k + 1, c0:c0 + ct])
                v_ref[s * seg + r0:s * seg + r0 + rc, c0:c0 + ct] = acc
    v = v_ref[...]
    mu = jnp.mean(v, axis=-1, keepdims=True)
    xc = v - mu
    y = xc * lax.rsqrt(jnp.mean(xc * xc, axis=-1, keepdims=True) + EPS) * lg_ref[...] + lb_ref[...]
    o_ref[...] = (_silu(y) * _silu(zb_ref[...].astype(F32))).astype(o_ref.dtype)


def _conformer(proj, conv_w, conv_b, ln_g, ln_b, *, row_off, rows, seg, glu_off, zb_off):
    taps, cf = conv_w.shape
    assert taps // 2 <= HALO
    tb = seg if seg >= 128 else 128
    nseg = tb // seg
    roff = row_off // tb
    kern = functools.partial(_conformer_kernel, seg=seg, nseg=nseg, cf=cf)
    return pl.pallas_call(
        kern,
        out_shape=jax.ShapeDtypeStruct((rows, cf), BF16),
        grid=(rows // tb,),
        in_specs=[pl.BlockSpec((tb, 2 * cf), lambda i: (i + roff, glu_off // (2 * cf))),
                  pl.BlockSpec((tb, cf), lambda i: (i + roff, zb_off // cf)),
                  pl.BlockSpec((taps, cf), lambda i: (0, 0)),
                  pl.BlockSpec((1, cf), lambda i: (0, 0)),
                  pl.BlockSpec((1, cf), lambda i: (0, 0)),
                  pl.BlockSpec((1, cf), lambda i: (0, 0))],
        out_specs=pl.BlockSpec((tb, cf), lambda i: (i, 0)),
        scratch_shapes=[pltpu.VMEM((nseg, seg + 2 * HALO, cf), F32), pltpu.VMEM((tb, cf), F32)],
        compiler_params=_params("parallel"),
        name="conformer",
    )(proj, proj, conv_w, conv_b.reshape(1, cf), ln_g.reshape(1, cf), ln_b.reshape(1, cf))


def _dft_cs(n):
    idx = np.arange(n)
    ang = 2.0 * np.pi * ((idx[:, None] * idx[None, :]) % n) / n
    return np.cos(ang), np.sin(ang)


def _fft_tables(l, gd):
    l1 = FFT_L1
    l2 = l // l1
    s = SUBLANES
    eye = jnp.eye(s, dtype=F32)
    c1, s1 = _dft_cs(l1)
    c2, s2 = _dft_cs(l2)

    def kron_a(m):
        m4 = jnp.asarray(m, F32).reshape(l1 // s, 1, s, l1, 1)
        return (m4 * eye.reshape(1, s, 1, 1, s)).reshape(l1 * s, l1 * s)

    def kron_b(m):
        m4 = jnp.asarray(m, F32).reshape(l2, 1, l2, 1)
        return (m4 * eye.reshape(1, s, 1, s)).reshape(l2 * s, l2 * s)

    ka = jnp.concatenate([kron_a(c1), kron_a(s1)], axis=0).astype(BF16)
    kc, ks = kron_b(c2), kron_b(s2)
    kb = jnp.concatenate([jnp.concatenate([kc, -ks], axis=1),
                          jnp.concatenate([ks, kc], axis=1)], axis=0).astype(BF16)
    k1 = np.arange(l1)[:, None]
    t2 = np.arange(l2)[None, :]
    ang = 2.0 * np.pi * ((k1 * t2) % l) / l

    def tw(m):
        m3 = np.asarray(m, np.float32).reshape(l1 // s, s, l2).transpose(0, 2, 1).reshape(l1 // s, l2 * s, 1)
        return jnp.broadcast_to(jnp.asarray(m3), (l1 // s, l2 * s, LANES))

    return ka, kb, tw(np.cos(ang)), tw(np.sin(ang)), _chan_table(l, gd)


def _chan_table(l, gd):
    cc, sc = _dft_cs(gd)
    return jnp.asarray(np.concatenate([cc, -sc], axis=0) / math.sqrt(l * gd), F32).astype(BF16)


def _fft_a_kernel(v_ref, ka_ref, or_ref, oi_ref):
    l1, s, cb = v_ref.shape
    xb = v_ref[...].reshape(l1 * s, cb).astype(BF16)
    y = jnp.dot(ka_ref[...], xb, preferred_element_type=F32)
    or_ref[...] = y[0:l1 * s].reshape(or_ref.shape)
    oi_ref[...] = y[l1 * s:2 * l1 * s].reshape(oi_ref.shape)


def _fft_b_kernel(ar_ref, ai_ref, twc_ref, tws_ref, kb_ref, tc_ref, o_ref):
    l2, s, cb = ar_ref.shape
    rows = l2 * s
    ar = ar_ref[...].reshape(rows, cb)
    ai = ai_ref[...].reshape(rows, cb)
    reps = cb // LANES
    c = jnp.tile(twc_ref[...], (1, reps))
    sn = jnp.tile(tws_ref[...], (1, reps))
    br = ar * c - ai * sn
    bi = ar * sn + ai * c
    z = jnp.dot(kb_ref[...], jnp.concatenate([br, bi], axis=0).astype(BF16),
                preferred_element_type=F32)
    xri = jnp.concatenate([z[0:rows], z[rows:2 * rows]], axis=1).astype(BF16)
    f = jnp.dot(xri, tc_ref[...], preferred_element_type=F32)
    o_ref[...] = f.reshape(o_ref.shape)


def _fourier_x(v, tables):
    l, c = v.shape
    ka, kb, twc, tws, tc = tables
    l1, s = FFT_L1, SUBLANES
    l2 = l // l1
    gd = c // FN_GROUPS
    nkb = l1 // s
    ap_shape = jax.ShapeDtypeStruct((nkb, l2, s, c), F32)
    apr, api = pl.pallas_call(
        _fft_a_kernel,
        out_shape=(ap_shape, ap_shape),
        grid=(l2 // s, c // gd),
        in_specs=[pl.BlockSpec((l1, s, gd), lambda i, j: (0, i, j)),
                  pl.BlockSpec(ka.shape, lambda i, j: (0, 0))],
        out_specs=(pl.BlockSpec((nkb, s, s, gd), lambda i, j: (0, i, 0, j)),
                   pl.BlockSpec((nkb, s, s, gd), lambda i, j: (0, i, 0, j))),
        compiler_params=_params("parallel", "parallel"),
        name="fft_stage_a",
    )(v.reshape(l1, l2, c), ka)
    f = pl.pallas_call(
        _fft_b_kernel,
        out_shape=jax.ShapeDtypeStruct((l2, nkb, s, c), F32),
        grid=(nkb, c // gd),
        in_specs=[pl.BlockSpec((None, l2, s, gd), lambda kq, j: (kq, 0, 0, j)),
                  pl.BlockSpec((None, l2, s, gd), lambda kq, j: (kq, 0, 0, j)),
                  pl.BlockSpec((None, l2 * s, LANES), lambda kq, j: (kq, 0, 0)),
                  pl.BlockSpec((None, l2 * s, LANES), lambda kq, j: (kq, 0, 0)),
                  pl.BlockSpec(kb.shape, lambda kq, j: (0, 0)),
                  pl.BlockSpec(tc.shape, lambda kq, j: (0, 0))],
        out_specs=pl.BlockSpec((l2, None, s, gd), lambda kq, j: (0, kq, 0, j)),
        compiler_params=_params("parallel", "parallel"),
        name="fft_stage_b",
    )(apr, api, twc, tws, kb, tc)
    return f.reshape(l, c)


def _gate_mul_kernel(f_ref, z_ref, o_ref):
    o_ref[...] = (f_ref[...] * _silu(z_ref[...].astype(F32))).astype(o_ref.dtype)


def _gate_mul(f, proj, zc_off):
    l, c = f.shape
    tm = _pick_tile(l, 512, BF16_SUBLANES)
    return pl.pallas_call(
        _gate_mul_kernel,
        out_shape=jax.ShapeDtypeStruct((l, c), BF16),
        grid=(l // tm,),
        in_specs=[pl.BlockSpec((tm, c), lambda i: (i, 0)),
                  pl.BlockSpec((tm, c), lambda i: (i, zc_off // c))],
        out_specs=pl.BlockSpec((tm, c), lambda i: (i, 0)),
        compiler_params=_params("parallel"),
        name="fourier_gate",
    )(f, proj)


def _fft_ctx_kernel(v_ref, z_ref, cs_ref, tc_ref, o_ref):
    lc = v_ref.shape[0]
    p = jnp.dot(cs_ref[...], v_ref[...], preferred_element_type=F32)
    pri = jnp.concatenate([p[0:lc], p[lc:2 * lc]], axis=1).astype(BF16)
    f = jnp.dot(pri, tc_ref[...], preferred_element_type=F32)
    o_ref[...] = (f * _silu(z_ref[...].astype(F32))).astype(o_ref.dtype)


def _fourier_ctx(proj, *, row_off, lc, c, v_off, zc_off):
    gd = c // FN_GROUPS
    cc, sc = _dft_cs(lc)
    cs = jnp.asarray(np.concatenate([cc, sc], axis=0), F32).astype(BF16)
    tc = _chan_table(lc, gd)
    rblk = row_off // lc
    return pl.pallas_call(
        _fft_ctx_kernel,
        out_shape=jax.ShapeDtypeStruct((lc, c), BF16),
        grid=(FN_GROUPS,),
        in_specs=[pl.BlockSpec((lc, gd), lambda g: (rblk, v_off // gd + g)),
                  pl.BlockSpec((lc, gd), lambda g: (rblk, zc_off // gd + g)),
                  pl.BlockSpec(cs.shape, lambda g: (0, 0)),
                  pl.BlockSpec(tc.shape, lambda g: (0, 0))],
        out_specs=pl.BlockSpec((lc, gd), lambda g: (0, g)),
        compiler_params=_params("parallel"),
        name="fourier_ctx",
    )(proj, proj, cs, tc)


def _merge_kernel(ya_ref, yb_ref, yc_ref, ga_ref, gb_ref, gc_ref, wa_ref, wb_ref, wc_ref, o_ref):
    def term(y_ref, w_ref, g_ref):
        return jax.nn.sigmoid(g_ref[...].astype(F32)) * jnp.dot(
            y_ref[...], w_ref[...], preferred_element_type=F32)
    acc = term(ya_ref, wa_ref, ga_ref) + term(yb_ref, wb_ref, gb_ref) + term(yc_ref, wc_ref, gc_ref)
    o_ref[...] = acc.astype(o_ref.dtype)


def _merge(ya, yb, yc, proj, w_branch, *, rows, ya_row_off, proj_row_off, gate_off):
    d = ya.shape[1]
    cf, fw = yb.shape[1], yc.shape[1]
    tm = _pick_tile(rows, 512, LANES)
    tn = _pick_tile(d, 512, LANES)
    ra, rpj = ya_row_off // tm, proj_row_off // tm
    gblk = gate_off // tn
    dblk = d // tn
    return pl.pallas_call(
        _merge_kernel,
        out_shape=jax.ShapeDtypeStruct((rows, d), BF16),
        grid=(rows // tm, d // tn),
        in_specs=[pl.BlockSpec((tm, d), lambda i, j: (i + ra, 0)),
                  pl.BlockSpec((tm, cf), lambda i, j: (i, 0)),
                  pl.BlockSpec((tm, fw), lambda i, j: (i, 0)),
                  pl.BlockSpec((tm, tn), lambda i, j: (i + rpj, gblk + j)),
                  pl.BlockSpec((tm, tn), lambda i, j: (i + rpj, gblk + dblk + j)),
                  pl.BlockSpec((tm, tn), lambda i, j: (i + rpj, gblk + 2 * dblk + j)),
                  pl.BlockSpec((d, tn), lambda i, j: (0, j)),
                  pl.BlockSpec((cf, tn), lambda i, j: (d // cf, j)),
                  pl.BlockSpec((fw, tn), lambda i, j: ((d + cf) // fw, j))],
        out_specs=pl.BlockSpec((tm, tn), lambda i, j: (i, j)),
        compiler_params=_params("parallel", "parallel"),
        name="merge",
    )(ya, yb, yc, proj, proj, proj, w_branch, w_branch, w_branch)


def _column_layout(d, cf, fw, heads):
    gn = SSD_GROUPS * SSD_STATE
    cc = d + 2 * gn
    src = {}
    pos = 0
    for name, width in (("xbc", cc), ("za", d), ("dt", 2 * heads), ("glu", 2 * cf), ("zb", cf),
                        ("v", fw), ("zc", fw), ("gates", 3 * d)):
        src[name] = (pos, width)
        pos += width
    order = (("xbc", cc), ("zb", cf), ("v", fw), ("zc", fw), ("za", d), ("glu", 2 * cf),
             ("gates", d), ("dt", 2 * heads))
    dst = {}
    pieces = []
    pos = 0
    for name, align in order:
        padw = (-pos) % align
        if padw:
            pieces.append(("pad", padw))
            pos += padw
        dst[name] = pos
        pieces.append((name, src[name]))
        pos += src[name][1]
    return dst, pieces, pos


def _permute_w_in(w, pieces):
    cols = []
    for name, spec in pieces:
        if name == "pad":
            cols.append(jnp.zeros((w.shape[0], spec), w.dtype))
        else:
            cols.append(w[:, spec[0]:spec[0] + spec[1]])
    return jnp.concatenate(cols, axis=1).astype(BF16)


def kernel(x, c, ctx, c_ctx, w_mod, b_mod, norm_g, w_in, ssd_conv_w, ssd_conv_b, ssd_dt_bias,
           ssd_a_log, ssd_d, ssd_norm_g, cf_conv_w, cf_conv_b, cf_ln_g, cf_ln_b, w_branch,
           w_out, final_g):
    bsz, l, d = x.shape
    assert bsz == 1, "single-sample kernel"
    lc = ctx.shape[1]
    depth = w_mod.shape[0]
    heads = ssd_dt_bias.shape[-1]
    cf = cf_conv_w.shape[-1]
    fw = w_branch.shape[1] - d - cf
    p = d // heads
    assert 2 * p == LANES and l % SSD_CHUNK == 0 and lc % SSD_CHUNK == 0 and l % lc == 0
    mt = l + lc
    nbx, nb = l // SSD_CHUNK, mt // SSD_CHUNK

    dst, pieces, ncols = _column_layout(d, cf, fw, heads)
    main_cols = dst["dt"]

    cond8 = jnp.zeros((SUBLANES, d), F32).at[0].set(c[0]).at[1].set(c_ctx)
    mod_all = _modulation(cond8, w_mod, b_mod)

    order_f = jnp.asarray(list(range(nbx, nb)) + list(range(nbx)), jnp.int32)
    order_b = jnp.asarray(list(range(nb - 1, nbx - 1, -1)) + list(range(nbx - 1, -1, -1)), jnp.int32)
    tables = _fft_tables(l, fw // FN_GROUPS)

    xs2 = x[0]
    cs2 = ctx[0]
    for i in range(depth):
        update_ctx = i < depth - 1
        w_in_p = _permute_w_in(w_in[i], pieces)
        w_br = w_branch[i].astype(BF16)
        w_o = w_out[i].astype(BF16)
        hcat = jnp.concatenate([_norm_mod(xs2, norm_g[i], mod_all, i, 0),
                                _norm_mod(cs2, norm_g[i], mod_all, i, 1)], axis=0)
        proj = _matmul(hcat, w_in_p, BF16, rows=mt, cols=main_cols, col_off=0,
                       tm_cap=1024, tn_cap=1024, name="in_proj")
        dt_raw = _matmul(hcat, w_in_p, F32, rows=mt, cols=2 * heads, col_off=dst["dt"],
                         tm_cap=1024, tn_cap=LANES, name="in_proj_dt")
        v_x = _matmul(hcat, w_in_p, F32, rows=l, cols=fw, col_off=dst["v"],
                      tm_cap=1024, tn_cap=1024, name="in_proj_v")

        xs_g, bt_g, c_g, cum, cumt, dtt, wt = _ssd_pre(
            proj, dt_raw, ssd_conv_w[i], ssd_conv_b[i], ssd_dt_bias[i].reshape(-1),
            ssd_a_log[i].reshape(-1), nx=l, d=d, heads=heads)
        y_f = _ssd_sweep(order_f, xs_g, bt_g, c_g, cum, cumt, dtt, wt, rev=False, heads=heads, d=d)
        dexp = jnp.repeat(ssd_d[i], p).reshape(1, d)
        ya = _ssd_sweep(order_b, xs_g, bt_g, c_g, cum, cumt, dtt, wt, rev=True, heads=heads, d=d,
                        extra=(y_f, proj, dst["za"] // d, dexp, ssd_norm_g[i].reshape(1, d)))

        yb = _conformer(proj, cf_conv_w[i], cf_conv_b[i], cf_ln_g[i], cf_ln_b[i],
                        row_off=0, rows=l, seg=GRID_W, glu_off=dst["glu"], zb_off=dst["zb"])
        yc = _gate_mul(_fourier_x(v_x, tables), proj, dst["zc"])
        merged = _merge(ya, yb, yc, proj, w_br, rows=l, ya_row_off=0, proj_row_off=0,
                        gate_off=dst["gates"])
        xs2 = _out_proj(merged, w_o, xs2, mod_all, i, 0)
        if update_ctx:
            yb_c = _conformer(proj, cf_conv_w[i], cf_conv_b[i], cf_ln_g[i], cf_ln_b[i],
                              row_off=l, rows=lc, seg=lc, glu_off=dst["glu"], zb_off=dst["zb"])
            yc_c = _fourier_ctx(proj, row_off=l, lc=lc, c=fw, v_off=dst["v"], zc_off=dst["zc"])
            merged_c = _merge(ya, yb_c, yc_c, proj, w_br, rows=lc, ya_row_off=l, proj_row_off=l,
                              gate_off=dst["gates"])
            cs2 = _out_proj(merged_c, w_o, cs2, mod_all, i, 1)
    return _final_norm(xs2, final_g)[None]
```

```python
import functools
import math

import numpy as np
import jax
import jax.numpy as jnp
from jax import lax
from jax.experimental import pallas as pl
from jax.experimental.pallas import tpu as pltpu

F32 = jnp.float32
BF16 = jnp.bfloat16

SSD_GROUPS = 8
SSD_STATE = 128
SSD_CHUNK = 128
GRID_W = 64
FN_GROUPS = 8
EPS = 1e-6
FFT_L1 = 64
LOG2E = math.log2(math.e)

SUBLANES = 8
LANES = 128
BF16_SUBLANES = 16
VMEM_LIMIT_BYTES = 56 * 1024 * 1024
HALO = 16


def _params(*sem):
    return pltpu.CompilerParams(dimension_semantics=sem, vmem_limit_bytes=VMEM_LIMIT_BYTES)


def _pick_tile(n, cap, mult):
    best = None
    t = mult
    while t <= min(n, cap):
        if n % t == 0:
            best = t
        t += mult
    assert best is not None, (n, cap, mult)
    return best


def _silu(v):
    return v * jax.nn.sigmoid(v)


def _mod_kernel(s_ref, w_ref, b_ref, o_ref):
    s = _silu(s_ref[...])
    o_ref[...] = jnp.dot(s.astype(BF16), w_ref[...].astype(BF16),
                         preferred_element_type=F32) + b_ref[...]


def _modulation(cond8, w_mod, b_mod):
    depth, d, n3 = w_mod.shape
    tn = _pick_tile(n3, 512, LANES)
    return pl.pallas_call(
        _mod_kernel,
        out_shape=jax.ShapeDtypeStruct((depth, SUBLANES, n3), F32),
        grid=(depth, n3 // tn),
        in_specs=[pl.BlockSpec((SUBLANES, d), lambda l, j: (0, 0)),
                  pl.BlockSpec((None, d, tn), lambda l, j: (l, 0, j)),
                  pl.BlockSpec((None, 1, tn), lambda l, j: (l, 0, j))],
        out_specs=pl.BlockSpec((None, SUBLANES, tn), lambda l, j: (l, 0, j)),
        compiler_params=_params("parallel", "parallel"),
        name="modulation",
    )(cond8, w_mod, b_mod.reshape(depth, 1, n3))


def _norm_mod_kernel(x_ref, c_ref, g_ref, sh_ref, sc_ref, o_ref, *, nbx):
    def emit(v_ref, row):
        v = v_ref[...]
        y = v * lax.rsqrt(jnp.mean(v * v, axis=-1, keepdims=True) + EPS) * g_ref[...]
        o_ref[...] = (y * (1.0 + sc_ref[row:row + 1, :]) + sh_ref[row:row + 1, :]).astype(o_ref.dtype)

    is_x = pl.program_id(0) < nbx
    pl.when(is_x)(lambda: emit(x_ref, 0))
    pl.when(jnp.logical_not(is_x))(lambda: emit(c_ref, 1))


def _norm_mod(x2, c2, g, mod_all, layer):
    l, d = x2.shape
    lc = c2.shape[0]
    tm = _pick_tile(math.gcd(l, lc), 256, BF16_SUBLANES)
    nbx, nbc = l // tm, lc // tm
    return pl.pallas_call(
        functools.partial(_norm_mod_kernel, nbx=nbx),
        out_shape=jax.ShapeDtypeStruct((l + lc, d), BF16),
        grid=(nbx + nbc,),
        in_specs=[pl.BlockSpec((tm, d), lambda i: (jnp.minimum(i, nbx - 1), 0)),
                  pl.BlockSpec((tm, d), lambda i: (jnp.maximum(i - nbx, 0), 0)),
                  pl.BlockSpec((1, d), lambda i: (0, 0)),
                  pl.BlockSpec((None, SUBLANES, d), lambda i: (layer, 0, 0)),
                  pl.BlockSpec((None, SUBLANES, d), lambda i: (layer, 0, 1))],
        out_specs=pl.BlockSpec((tm, d), lambda i: (i, 0)),
        compiler_params=_params("parallel"),
        name="norm_mod",
    )(x2, c2, g.reshape(1, d), mod_all, mod_all)


def _final_norm_kernel(x_ref, g_ref, o_ref):
    x = x_ref[...]
    o_ref[...] = x * lax.rsqrt(jnp.mean(x * x, axis=-1, keepdims=True) + EPS) * g_ref[...]


def _final_norm(x2, g):
    m, d = x2.shape
    tm = _pick_tile(m, 256, SUBLANES)
    return pl.pallas_call(
        _final_norm_kernel,
        out_shape=jax.ShapeDtypeStruct((m, d), F32),
        grid=(m // tm,),
        in_specs=[pl.BlockSpec((tm, d), lambda i: (i, 0)),
                  pl.BlockSpec((1, d), lambda i: (0, 0))],
        out_specs=pl.BlockSpec((tm, d), lambda i: (i, 0)),
        compiler_params=_params("parallel"),
        name="final_norm",
    )(x2, g.reshape(1, d))


def _in_proj_kernel(src_ref, a_ref, w_ref, o_ref, wb_ref):
    del src_ref

    @pl.when(pl.program_id(1) == 0)
    def _():
        wb_ref[...] = w_ref[...].astype(BF16)

    o_ref[...] = jnp.dot(a_ref[...], wb_ref[...], preferred_element_type=F32).astype(o_ref.dtype)


def _in_proj(a, w_in, layer, src_cols, tn, out_dtype, name):
    m, k = a.shape
    nblk = len(src_cols)
    tm = _pick_tile(m, 1024, LANES)
    assert all(c % LANES == 0 for c in src_cols) or tn < LANES
    unit = LANES if tn >= LANES else 1
    src = jnp.asarray([c // unit for c in src_cols], jnp.int32)
    return pl.pallas_call(
        _in_proj_kernel,
        out_shape=jax.ShapeDtypeStruct((m, nblk * tn), out_dtype),
        grid_spec=pltpu.PrefetchScalarGridSpec(
            num_scalar_prefetch=1, grid=(nblk, m // tm),
            in_specs=[pl.BlockSpec((tm, k), lambda j, i, s: (i, 0)),
                      pl.BlockSpec((None, pl.Element(k), pl.Element(tn)),
                                   lambda j, i, s: (layer, 0, s[j] * unit))],
            out_specs=pl.BlockSpec((tm, tn), lambda j, i, s: (i, j)),
            scratch_shapes=[pltpu.VMEM((k, tn), BF16)]),
        compiler_params=_params("parallel", "arbitrary"),
        name=name,
    )(src, a, w_in)


def _out_proj_kernel(a_ref, w_ref, r_ref, g_ref, o_ref, wb_ref, *, row):
    @pl.when(pl.program_id(1) == 0)
    def _():
        wb_ref[...] = w_ref[...].astype(BF16)

    acc = jnp.dot(a_ref[...], wb_ref[...], preferred_element_type=F32)
    o_ref[...] = r_ref[...] + g_ref[row:row + 1, :] * acc


def _out_proj(merged, w_out, resid, mod_all, layer, row):
    m, d = resid.shape
    tm = _pick_tile(m, 1024, LANES)
    tn = _pick_tile(d, 512, LANES)
    goff = 2 * d // tn
    return pl.pallas_call(
        functools.partial(_out_proj_kernel, row=row),
        out_shape=jax.ShapeDtypeStruct((m, d), F32),
        grid=(d // tn, m // tm),
        in_specs=[pl.BlockSpec((tm, d), lambda j, i: (i, 0)),
                  pl.BlockSpec((None, d, tn), lambda j, i: (layer, 0, j)),
                  pl.BlockSpec((tm, tn), lambda j, i: (i, j)),
                  pl.BlockSpec((None, SUBLANES, tn), lambda j, i: (layer, 0, j + goff))],
        out_specs=pl.BlockSpec((tm, tn), lambda j, i: (i, j)),
        scratch_shapes=[pltpu.VMEM((d, tn), BF16)],
        compiler_params=_params("parallel", "arbitrary"),
        name="out_proj",
    )(merged, w_out, resid, mod_all)


def _split3(a):
    hi = a.astype(BF16)
    r1 = a - hi.astype(F32)
    mid = r1.astype(BF16)
    lo = (r1 - mid.astype(F32)).astype(BF16)
    return hi, mid, lo


def _ssd_pre_kernel(cur_ref, prev_ref, next_ref, dtraw_ref, cw_ref, cb_ref, dtb_ref, alog_ref,
                    xs_ref, bt_ref, c_ref, cumt_ref, rdt_ref, wt_ref, ext_ref,
                    *, nbx, nb, d, rp, heads):
    q = SSD_CHUNK
    n = SSD_STATE
    i = pl.program_id(0)
    first = jnp.logical_or(i == 0, i == nbx)
    last = jnp.logical_or(i == nbx - 1, i == nb - 1)
    ext_ref[HALO:HALO + q, :] = cur_ref[...].astype(F32)
    ext_ref[0:HALO, :] = prev_ref[...].astype(F32) * jnp.where(first, 0.0, 1.0)
    ext_ref[HALO + q:2 * HALO + q, :] = next_ref[...].astype(F32) * jnp.where(last, 0.0, 1.0)
    taps = cw_ref.shape[0]
    half = taps // 2

    def conv_silu(c0, width):
        acc = jnp.broadcast_to(cb_ref[:, c0:c0 + width], (q, width))
        for k in range(taps):
            acc = acc + ext_ref[pl.ds(HALO - half + k, q), c0:c0 + width] * cw_ref[k:k + 1, c0:c0 + width]
        return _silu(acc)

    for g in range(SSD_GROUPS):
        xs_ref[g] = conv_silu(g * rp, rp).astype(xs_ref.dtype)
    for g in range(SSD_GROUPS):
        bt_ref[g] = conv_silu(d + g * n, n).T.astype(bt_ref.dtype)
    for g in range(SSD_GROUPS):
        c_ref[g] = conv_silu(d + SSD_GROUPS * n + g * n, n).astype(c_ref.dtype)

    raw = dtraw_ref[...] + dtb_ref[...]
    dt = jnp.maximum(raw, 0.0) + jnp.log1p(jnp.exp(-jnp.abs(raw)))
    a = dt * (-jnp.exp(alog_ref[...]))
    row = lax.broadcasted_iota(jnp.int32, (q, q), 0)
    col = lax.broadcasted_iota(jnp.int32, (q, q), 1)
    tri_f = jnp.where(col <= row, 1.0, 0.0).astype(BF16)
    tri_b = jnp.where(col >= row, 1.0, 0.0).astype(BF16)
    cf = jnp.zeros((q, 2 * heads), F32)
    cb = jnp.zeros((q, 2 * heads), F32)
    for piece in _split3(a):
        cf = cf + jnp.dot(tri_f, piece, preferred_element_type=F32)
        cb = cb + jnp.dot(tri_b, piece, preferred_element_type=F32)
    is_fwd = lax.broadcasted_iota(jnp.int32, (q, 2 * heads), 1) < heads
    cum2 = jnp.where(is_fwd, cf, cb) * LOG2E
    clast = jnp.where(is_fwd[0:1, :], cum2[q - 1:q, :], cum2[0:1, :])
    w = dt * jnp.exp2(clast - cum2)
    cumt_ref[...] = cum2.T
    rdt_ref[...] = (cum2 - jnp.log(dt) * LOG2E).T
    wt_ref[...] = w.T


def _ssd_pre(proj, dt_raw, conv_w, conv_b, dt_bias, a_log, *, nx, d, heads):
    mt = proj.shape[0]
    q, n, g = SSD_CHUNK, SSD_STATE, SSD_GROUPS
    cc = d + 2 * g * n
    rp = d // g
    nb, nbx = mt // q, nx // q
    per = q // HALO
    nhb = mt // HALO
    h2 = 2 * heads
    kern = functools.partial(_ssd_pre_kernel, nbx=nbx, nb=nb, d=d, rp=rp, heads=heads)
    return pl.pallas_call(
        kern,
        out_shape=(jax.ShapeDtypeStruct((g, mt, rp), BF16),
                   jax.ShapeDtypeStruct((g, n, mt), BF16),
                   jax.ShapeDtypeStruct((g, mt, n), BF16),
                   jax.ShapeDtypeStruct((nb, h2, q), F32),
                   jax.ShapeDtypeStruct((nb, h2, q), F32),
                   jax.ShapeDtypeStruct((nb, h2, q), F32)),
        grid=(nb,),
        in_specs=[pl.BlockSpec((q, cc), lambda i: (i, 0)),
                  pl.BlockSpec((HALO, cc), lambda i: (jnp.maximum(i * per - 1, 0), 0)),
                  pl.BlockSpec((HALO, cc), lambda i: (jnp.minimum((i + 1) * per, nhb - 1), 0)),
                  pl.BlockSpec((q, h2), lambda i: (i, 0)),
                  pl.BlockSpec(conv_w.shape, lambda i: (0, 0)),
                  pl.BlockSpec((1, cc), lambda i: (0, 0)),
                  pl.BlockSpec((1, h2), lambda i: (0, 0)),
                  pl.BlockSpec((1, h2), lambda i: (0, 0))],
        out_specs=(pl.BlockSpec((g, q, rp), lambda i: (0, i, 0)),
                   pl.BlockSpec((g, n, q), lambda i: (0, 0, i)),
                   pl.BlockSpec((g, q, n), lambda i: (0, i, 0)),
                   pl.BlockSpec((None, h2, q), lambda i: (i, 0, 0)),
                   pl.BlockSpec((None, h2, q), lambda i: (i, 0, 0)),
                   pl.BlockSpec((None, h2, q), lambda i: (i, 0, 0))),
        scratch_shapes=[pltpu.VMEM((q + 2 * HALO, cc), F32)],
        compiler_params=_params("parallel"),
        name="ssd_pre",
    )(proj, proj, proj, dt_raw, conv_w, conv_b.reshape(1, cc), dt_bias.reshape(1, h2),
      a_log.reshape(1, h2))


def _ssd_sweep_kernel(order_ref, xs_ref, bt_ref, c_ref, cumt_ref, rdt_ref, wt_ref, *rest,
                      rev, heads, hpg, d):
    del order_ref
    q, n, g_cnt = SSD_CHUNK, SSD_STATE, SSD_GROUPS
    if rev:
        yf_ref, za_ref, dexp_ref, ng_ref, out_ref, h_ref, yall_ref = rest
    else:
        out_ref, h_ref = rest
    pw = LANES
    p = pw // 2
    rp = hpg * p
    hoff = heads if rev else 0

    @pl.when(pl.program_id(0) == 0)
    def _():
        h_ref[...] = jnp.zeros_like(h_ref)

    row = lax.broadcasted_iota(jnp.int32, (q, q), 0)
    col = lax.broadcasted_iota(jnp.int32, (q, q), 1)
    tri = (col >= row) if rev else (col <= row)
    left = lax.broadcasted_iota(jnp.int32, (q, pw), 1) < p
    left1 = left[0:1, :]
    edge = 0 if rev else q - 1

    def group_body(g, carry):
        xs_g = xs_ref[g]
        c_g = c_ref[g]
        bt_g = bt_ref[g]
        s_mat = jnp.dot(c_g, bt_g, preferred_element_type=F32)
        c_f = c_g.astype(F32)
        bt_f = bt_g.astype(F32)
        for pair in range(hpg // 2):
            sl = slice(pair * pw, (pair + 1) * pw)
            xs2 = xs_g[:, sl]
            zero_x = jnp.zeros_like(xs2)
            xs_l = jnp.where(left, xs2, zero_x)
            xs_r = jnp.where(left, zero_x, xs2)
            h2 = h_ref[g, :, sl]
            h2b = h2.astype(BF16)
            zero_h = jnp.zeros_like(h2b)
            hs_l = jnp.where(left, h2b, zero_h)
            hs_r = jnp.where(left, zero_h, h2b)
            w_l, cs_l, bts_l, cd_l = [], [], [], []
            for e in range(2):
                hidx = hoff + g * hpg + 2 * pair + e
                colb = jnp.broadcast_to(cumt_ref[pl.ds(hidx, 1), :], (q, q)).T
                rowq = rdt_ref[pl.ds(hidx, 1), :]
                roww = wt_ref[pl.ds(hidx, 1), :]
                dec = jnp.exp2(jnp.where(tri, colb - rowq, -1e30))
                w_l.append((s_mat * dec).astype(BF16))
                ecol = jnp.exp2(colb)
                cs_l.append((c_f * ecol).astype(BF16))
                bts_l.append((bt_f * roww).astype(BF16))
                cd_l.append(ecol[edge:edge + 1, :])
            lhs_y = jnp.concatenate(w_l + cs_l, axis=1)
            rhs_y = jnp.concatenate([xs_l, xs_r, hs_l, hs_r], axis=0)
            y2 = jnp.dot(lhs_y, rhs_y, preferred_element_type=F32)
            st2 = jnp.dot(jnp.concatenate(bts_l, axis=1), jnp.concatenate([xs_l, xs_r], axis=0),
                          preferred_element_type=F32)
            cd2 = jnp.where(left1, cd_l[0], cd_l[1])
            h_ref[g, :, sl] = h2 * cd2 + st2
            if rev:
                yall_ref[g, :, sl] = y2 + yf_ref[g, :, sl]
            else:
                out_ref[g, :, sl] = y2
        return carry

    lax.fori_loop(0, g_cnt, group_body, 0, unroll=4)

    if rev:
        ssq = jnp.zeros((q, 1), F32)
        for g in range(g_cnt):
            gs = slice(g * rp, (g + 1) * rp)
            y = yall_ref[g] + dexp_ref[:, gs] * xs_ref[g].astype(F32)
            t = y * _silu(za_ref[:, gs].astype(F32))
            yall_ref[g] = t
            ssq = ssq + jnp.sum(t * t, axis=1, keepdims=True)
        scale = lax.rsqrt(ssq / d + EPS)
        for g in range(g_cnt):
            gs = slice(g * rp, (g + 1) * rp)
            out_ref[:, gs] = (yall_ref[g] * scale * ng_ref[:, gs]).astype(out_ref.dtype)


def _ssd_sweep(order, xs, bt, cm, cumt, rdt, wt, *, rev, heads, d, extra=None):
    g, mt, rp = xs.shape
    q, n = SSD_CHUNK, SSD_STATE
    nb = mt // q
    h2 = 2 * heads
    hpg = heads // g
    assert hpg % 2 == 0 and rp == hpg * (LANES // 2)
    in_specs = [pl.BlockSpec((g, q, rp), lambda s, o: (0, o[s], 0)),
                pl.BlockSpec((g, n, q), lambda s, o: (0, 0, o[s])),
                pl.BlockSpec((g, q, n), lambda s, o: (0, o[s], 0)),
                pl.BlockSpec((None, h2, q), lambda s, o: (o[s], 0, 0)),
                pl.BlockSpec((None, h2, q), lambda s, o: (o[s], 0, 0)),
                pl.BlockSpec((None, h2, q), lambda s, o: (o[s], 0, 0))]
    args = [xs, bt, cm, cumt, rdt, wt]
    scratch = [pltpu.VMEM((g, n, rp), F32)]
    if rev:
        yf, proj, za_blk, dexp, ng = extra
        in_specs += [pl.BlockSpec((g, q, rp), lambda s, o: (0, o[s], 0)),
                     pl.BlockSpec((q, d), lambda s, o: (o[s], za_blk)),
                     pl.BlockSpec((1, d), lambda s, o: (0, 0)),
                     pl.BlockSpec((1, d), lambda s, o: (0, 0))]
        args += [yf, proj, dexp, ng]
        out_shape = jax.ShapeDtypeStruct((mt, d), BF16)
        out_spec = pl.BlockSpec((q, d), lambda s, o: (o[s], 0))
        scratch.append(pltpu.VMEM((g, q, rp), F32))
    else:
        out_shape = jax.ShapeDtypeStruct((g, mt, rp), F32)
        out_spec = pl.BlockSpec((g, q, rp), lambda s, o: (0, o[s], 0))
    kern = functools.partial(_ssd_sweep_kernel, rev=rev, heads=heads, hpg=hpg, d=d)
    return pl.pallas_call(
        kern,
        out_shape=out_shape,
        grid_spec=pltpu.PrefetchScalarGridSpec(
            num_scalar_prefetch=1, grid=(nb,), in_specs=in_specs, out_specs=out_spec,
            scratch_shapes=scratch),
        compiler_params=_params("arbitrary"),
        name="ssd_sweep_bwd" if rev else "ssd_sweep_fwd",
    )(order, *args)


def _conformer_kernel(glu_ref, zb_ref, cw_ref, cb_ref, lg_ref, lb_ref, o_ref, pad_ref, v_ref,
                      *, seg, nseg, cf):
    taps = cw_ref.shape[0]
    half = taps // 2
    rc = min(seg, 64)
    ct = 2 * LANES
    u = glu_ref[:, 0:cf].astype(F32) * jax.nn.sigmoid(glu_ref[:, cf:2 * cf].astype(F32))
    zlo = jnp.zeros((HALO, cf), F32)
    zhi = jnp.zeros((HALO + SUBLANES, cf), F32)
    for ph in range(SUBLANES):
        for s in range(nseg):
            pad_ref[ph, s, 0:HALO, :] = zlo
            pad_ref[ph, s, seg + HALO - SUBLANES:seg + 2 * HALO, :] = zhi
            pad_ref[ph, s, HALO - ph:HALO - ph + seg, :] = u[s * seg:(s + 1) * seg, :]
    for c0 in range(0, cf, ct):
        for s in range(nseg):
            for r0 in range(0, seg, rc):
                acc = jnp.broadcast_to(cb_ref[:, c0:c0 + ct], (rc, ct))
                for k in range(taps):
                    off = HALO + k - half
                    ph = off % SUBLANES
                    base = off - ph + r0
                    acc = acc + pad_ref[ph, s, base:base + rc, c0:c0 + ct] * cw_ref[k:k + 1, c0:c0 + ct]
                v_ref[s * seg + r0:s * seg + r0 + rc, c0:c0 + ct] = acc
    v = v_ref[...]
    mu = jnp.mean(v, axis=-1, keepdims=True)
    xc = v - mu
    y = xc * lax.rsqrt(jnp.mean(xc * xc, axis=-1, keepdims=True) + EPS) * lg_ref[...] + lb_ref[...]
    o_ref[...] = (_silu(y) * _silu(zb_ref[...].astype(F32))).astype(o_ref.dtype)


def _conformer(proj, conv_w, conv_b, ln_g, ln_b, *, row_off, rows, seg, glu_off, zb_off):
    taps, cf = conv_w.shape
    assert taps // 2 <= HALO
    tb = seg if seg >= 128 else 128
    nseg = tb // seg
    roff = row_off // tb
    kern = functools.partial(_conformer_kernel, seg=seg, nseg=nseg, cf=cf)
    return pl.pallas_call(
        kern,
        out_shape=jax.ShapeDtypeStruct((rows, cf), BF16),
        grid=(rows // tb,),
        in_specs=[pl.BlockSpec((tb, 2 * cf), lambda i: (i + roff, glu_off // (2 * cf))),
                  pl.BlockSpec((tb, cf), lambda i: (i + roff, zb_off // cf)),
                  pl.BlockSpec((taps, cf), lambda i: (0, 0)),
                  pl.BlockSpec((1, cf), lambda i: (0, 0)),
                  pl.BlockSpec((1, cf), lambda i: (0, 0)),
                  pl.BlockSpec((1, cf), lambda i: (0, 0))],
        out_specs=pl.BlockSpec((tb, cf), lambda i: (i, 0)),
        scratch_shapes=[pltpu.VMEM((SUBLANES, nseg, seg + 2 * HALO, cf), F32),
                        pltpu.VMEM((tb, cf), F32)],
        compiler_params=_params("parallel"),
        name="conformer",
    )(proj, proj, conv_w, conv_b.reshape(1, cf), ln_g.reshape(1, cf), ln_b.reshape(1, cf))


def _dft_cs(n):
    idx = np.arange(n)
    ang = 2.0 * np.pi * ((idx[:, None] * idx[None, :]) % n) / n
    return np.cos(ang), np.sin(ang)


def _fft_tables(l, gd):
    l1 = FFT_L1
    l2 = l // l1
    s = SUBLANES
    eye = np.eye(s)
    c1, s1 = _dft_cs(l1)
    c2, s2 = _dft_cs(l2)

    def kron_a(m):
        m4 = m.reshape(l1 // s, 1, s, l1, 1)
        return (m4 * eye.reshape(1, s, 1, 1, s)).reshape(l1 * s, l1 * s)

    def kron_b(m):
        m4 = m.reshape(l2, 1, l2, 1)
        return (m4 * eye.reshape(1, s, 1, s)).reshape(l2 * s, l2 * s)

    def const_bf16(m):
        return jnp.asarray(m.astype(np.float32)).astype(BF16)

    ka = const_bf16(np.concatenate([kron_a(c1), kron_a(s1)], axis=0))
    kc, ks = kron_b(c2), kron_b(s2)
    kb = const_bf16(np.block([[kc, -ks], [ks, kc]]))
    k1 = np.arange(l1)[:, None]
    t2 = np.arange(l2)[None, :]
    ang = 2.0 * np.pi * ((k1 * t2) % l) / l

    def tw(m):
        m3 = m.astype(np.float32).reshape(l1 // s, s, l2).transpose(0, 2, 1).reshape(l1 // s, l2 * s, 1)
        return jnp.asarray(np.ascontiguousarray(np.broadcast_to(m3, (l1 // s, l2 * s, LANES))))

    return ka, kb, tw(np.cos(ang)), tw(np.sin(ang)), _chan_table(l, gd)


def _chan_table(l, gd):
    cc, sc = _dft_cs(gd)
    return jnp.asarray(np.concatenate([cc, -sc], axis=0) / math.sqrt(l * gd), F32).astype(BF16)


def _fft_a_kernel(v_ref, ka_ref, or_ref, oi_ref):
    l1, rows, cb = v_ref.shape
    s = SUBLANES
    nkb = or_ref.shape[0]
    xf = v_ref[...].astype(F32)
    for h in range(rows // s):
        xb = xf[:, h * s:(h + 1) * s, :].reshape(l1 * s, cb).astype(BF16)
        y = jnp.dot(ka_ref[...], xb, preferred_element_type=F32)
        or_ref[:, h * s:(h + 1) * s, :, :] = y[0:l1 * s].reshape(nkb, s, s, cb)
        oi_ref[:, h * s:(h + 1) * s, :, :] = y[l1 * s:2 * l1 * s].reshape(nkb, s, s, cb)


def _fft_b_kernel(ar_ref, ai_ref, twc_ref, tws_ref, kb_ref, tc_ref, o_ref):
    l2, s, cb = ar_ref.shape
    rows = l2 * s
    ar = ar_ref[...].reshape(rows, cb)
    ai = ai_ref[...].reshape(rows, cb)
    reps = cb // LANES
    c = jnp.tile(twc_ref[...], (1, reps))
    sn = jnp.tile(tws_ref[...], (1, reps))
    br = ar * c - ai * sn
    bi = ar * sn + ai * c
    z = jnp.dot(kb_ref[...], jnp.concatenate([br, bi], axis=0).astype(BF16),
                preferred_element_type=F32)
    xri = jnp.concatenate([z[0:rows], z[rows:2 * rows]], axis=1).astype(BF16)
    f = jnp.dot(xri, tc_ref[...], preferred_element_type=F32)
    o_ref[...] = f.reshape(o_ref.shape)


def _fourier_x(proj, tables, *, l, c, v_off):
    ka, kb, twc, tws, tc = tables
    l1, s = FFT_L1, SUBLANES
    l2 = l // l1
    gd = c // FN_GROUPS
    nkb = l1 // s
    mt, ncols = proj.shape
    tr = BF16_SUBLANES
    assert mt % l2 == 0 and l2 % tr == 0
    ap_shape = jax.ShapeDtypeStruct((nkb, l2, s, c), F32)
    voff = v_off // gd
    apr, api = pl.pallas_call(
        _fft_a_kernel,
        out_shape=(ap_shape, ap_shape),
        grid=(l2 // tr, c // gd),
        in_specs=[pl.BlockSpec((l1, tr, gd), lambda i, j: (0, i, voff + j)),
                  pl.BlockSpec(ka.shape, lambda i, j: (0, 0))],
        out_specs=(pl.BlockSpec((nkb, tr, s, gd), lambda i, j: (0, i, 0, j)),
                   pl.BlockSpec((nkb, tr, s, gd), lambda i, j: (0, i, 0, j))),
        compiler_params=_params("parallel", "parallel"),
        name="fft_stage_a",
    )(proj.reshape(mt // l2, l2, ncols), ka)
    f = pl.pallas_call(
        _fft_b_kernel,
        out_shape=jax.ShapeDtypeStruct((l2, nkb, s, c), F32),
        grid=(nkb, c // gd),
        in_specs=[pl.BlockSpec((None, l2, s, gd), lambda kq, j: (kq, 0, 0, j)),
                  pl.BlockSpec((None, l2, s, gd), lambda kq, j: (kq, 0, 0, j)),
                  pl.BlockSpec((None, l2 * s, LANES), lambda kq, j: (kq, 0, 0)),
                  pl.BlockSpec((None, l2 * s, LANES), lambda kq, j: (kq, 0, 0)),
                  pl.BlockSpec(kb.shape, lambda kq, j: (0, 0)),
                  pl.BlockSpec(tc.shape, lambda kq, j: (0, 0))],
        out_specs=pl.BlockSpec((l2, None, s, gd), lambda kq, j: (0, kq, 0, j)),
        compiler_params=_params("parallel", "parallel"),
        name="fft_stage_b",
    )(apr, api, twc, tws, kb, tc)
    return f.reshape(l, c)


def _gate_mul_kernel(f_ref, z_ref, o_ref):
    o_ref[...] = (f_ref[...] * _silu(z_ref[...].astype(F32))).astype(o_ref.dtype)


def _gate_mul(f, proj, zc_off):
    l, c = f.shape
    tm = _pick_tile(l, 512, BF16_SUBLANES)
    return pl.pallas_call(
        _gate_mul_kernel,
        out_shape=jax.ShapeDtypeStruct((l, c), BF16),
        grid=(l // tm,),
        in_specs=[pl.BlockSpec((tm, c), lambda i: (i, 0)),
                  pl.BlockSpec((tm, c), lambda i: (i, zc_off // c))],
        out_specs=pl.BlockSpec((tm, c), lambda i: (i, 0)),
        compiler_params=_params("parallel"),
        name="fourier_gate",
    )(f, proj)


def _fft_ctx_kernel(v_ref, z_ref, cs_ref, tc_ref, o_ref):
    lc = v_ref.shape[0]
    p = jnp.dot(cs_ref[...], v_ref[...], preferred_element_type=F32)
    pri = jnp.concatenate([p[0:lc], p[lc:2 * lc]], axis=1).astype(BF16)
    f = jnp.dot(pri, tc_ref[...], preferred_element_type=F32)
    o_ref[...] = (f * _silu(z_ref[...].astype(F32))).astype(o_ref.dtype)


def _fourier_ctx(proj, *, row_off, lc, c, v_off, zc_off):
    gd = c // FN_GROUPS
    cc, sc = _dft_cs(lc)
    cs = jnp.asarray(np.concatenate([cc, sc], axis=0), F32).astype(BF16)
    tc = _chan_table(lc, gd)
    rblk = row_off // lc
    return pl.pallas_call(
        _fft_ctx_kernel,
        out_shape=jax.ShapeDtypeStruct((lc, c), BF16),
        grid=(FN_GROUPS,),
        in_specs=[pl.BlockSpec((lc, gd), lambda g: (rblk, v_off // gd + g)),
                  pl.BlockSpec((lc, gd), lambda g: (rblk, zc_off // gd + g)),
                  pl.BlockSpec(cs.shape, lambda g: (0, 0)),
                  pl.BlockSpec(tc.shape, lambda g: (0, 0))],
        out_specs=pl.BlockSpec((lc, gd), lambda g: (0, g)),
        compiler_params=_params("parallel"),
        name="fourier_ctx",
    )(proj, proj, cs, tc)


def _merge_kernel(ya_ref, yb_ref, yc_ref, ga_ref, gb_ref, gc_ref, wa_ref, wb_ref, wc_ref, o_ref):
    def term(y_ref, w_ref, g_ref):
        return jax.nn.sigmoid(g_ref[...].astype(F32)) * jnp.dot(
            y_ref[...], w_ref[...], preferred_element_type=F32)
    acc = term(ya_ref, wa_ref, ga_ref) + term(yb_ref, wb_ref, gb_ref) + term(yc_ref, wc_ref, gc_ref)
    o_ref[...] = acc.astype(o_ref.dtype)


def _merge(ya, yb, yc, proj, w_branch, layer, *, rows, ya_row_off, proj_row_off, gate_off):
    d = ya.shape[1]
    cf, fw = yb.shape[1], yc.shape[1]
    tm = _pick_tile(rows, 512, LANES)
    tn = _pick_tile(d, 512, LANES)
    ra, rpj = ya_row_off // tm, proj_row_off // tm
    gblk = gate_off // tn
    dblk = d // tn
    return pl.pallas_call(
        _merge_kernel,
        out_shape=jax.ShapeDtypeStruct((rows, d), BF16),
        grid=(rows // tm, d // tn),
        in_specs=[pl.BlockSpec((tm, d), lambda i, j: (i + ra, 0)),
                  pl.BlockSpec((tm, cf), lambda i, j: (i, 0)),
                  pl.BlockSpec((tm, fw), lambda i, j: (i, 0)),
                  pl.BlockSpec((tm, tn), lambda i, j: (i + rpj, gblk + j)),
                  pl.BlockSpec((tm, tn), lambda i, j: (i + rpj, gblk + dblk + j)),
                  pl.BlockSpec((tm, tn), lambda i, j: (i + rpj, gblk + 2 * dblk + j)),
                  pl.BlockSpec((None, d, tn), lambda i, j: (layer, 0, j)),
                  pl.BlockSpec((None, cf, tn), lambda i, j: (layer, d // cf, j)),
                  pl.BlockSpec((None, fw, tn), lambda i, j: (layer, (d + cf) // fw, j))],
        out_specs=pl.BlockSpec((tm, tn), lambda i, j: (i, j)),
        compiler_params=_params("parallel", "parallel"),
        name="merge",
    )(ya, yb, yc, proj, proj, proj, w_branch, w_branch, w_branch)


def _column_layout(d, cf, fw, heads):
    gn = SSD_GROUPS * SSD_STATE
    cc = d + 2 * gn
    src = {}
    pos = 0
    for name, width in (("xbc", cc), ("za", d), ("dt", 2 * heads), ("glu", 2 * cf), ("zb", cf),
                        ("v", fw), ("zc", fw), ("gates", 3 * d)):
        src[name] = (pos, width)
        pos += width
    order = (("xbc", cc), ("zb", cf), ("v", fw), ("zc", fw), ("za", d), ("glu", 2 * cf),
             ("gates", d), ("dt", 2 * heads))
    dst = {}
    pieces = []
    pos = 0
    for name, align in order:
        padw = (-pos) % align
        if padw:
            pieces.append(("pad", padw))
            pos += padw
        dst[name] = pos
        pieces.append((name, src[name]))
        pos += src[name][1]
    return dst, pieces, pos


def _source_columns(pieces):
    main = [(name, spec) for name, spec in pieces if name != "dt"]
    widths = [spec if name == "pad" else spec[1] for name, spec in main]
    tn = _pick_tile(functools.reduce(math.gcd, widths), 512, LANES)
    cols = []
    for name, spec in main:
        start, width = (0, spec) if name == "pad" else spec
        cols += [start + k * tn if name != "pad" else 0 for k in range(width // tn)]
    return tn, cols


def kernel(x, c, ctx, c_ctx, w_mod, b_mod, norm_g, w_in, ssd_conv_w, ssd_conv_b, ssd_dt_bias,
           ssd_a_log, ssd_d, ssd_norm_g, cf_conv_w, cf_conv_b, cf_ln_g, cf_ln_b, w_branch,
           w_out, final_g):
    bsz, l, d = x.shape
    assert bsz == 1, "single-sample kernel"
    lc = ctx.shape[1]
    depth = w_mod.shape[0]
    heads = ssd_dt_bias.shape[-1]
    cf = cf_conv_w.shape[-1]
    fw = w_branch.shape[1] - d - cf
    p = d // heads
    assert 2 * p == LANES and l % SSD_CHUNK == 0 and lc % SSD_CHUNK == 0 and l % lc == 0
    mt = l + lc
    nbx, nb = l // SSD_CHUNK, mt // SSD_CHUNK

    dst, pieces, _ = _column_layout(d, cf, fw, heads)
    tn_in, src_cols = _source_columns(pieces)
    dt_src = [spec[0] for name, spec in pieces if name == "dt"]

    cond8 = jnp.zeros((SUBLANES, d), F32).at[0].set(c[0]).at[1].set(c_ctx)
    mod_all = _modulation(cond8, w_mod, b_mod)

    order_f = jnp.asarray(list(range(nbx, nb)) + list(range(nbx)), jnp.int32)
    order_b = jnp.asarray(list(range(nb - 1, nbx - 1, -1)) + list(range(nbx - 1, -1, -1)), jnp.int32)
    tables = _fft_tables(l, fw // FN_GROUPS)

    xs2 = x[0]
    cs2 = ctx[0]
    w_br = w_branch.astype(BF16)
    for i in range(depth):
        update_ctx = i < depth - 1
        hcat = _norm_mod(xs2, cs2, norm_g[i], mod_all, i)
        proj = _in_proj(hcat, w_in, i, src_cols, tn_in, BF16, "in_proj")
        dt_raw = _in_proj(hcat, w_in, i, dt_src, 2 * heads, F32, "in_proj_dt")

        xs_g, bt_g, c_g, cumt, rdt, wt = _ssd_pre(
            proj, dt_raw, ssd_conv_w[i], ssd_conv_b[i], ssd_dt_bias[i].reshape(-1),
            ssd_a_log[i].reshape(-1), nx=l, d=d, heads=heads)
        y_f = _ssd_sweep(order_f, xs_g, bt_g, c_g, cumt, rdt, wt, rev=False, heads=heads, d=d)
        dexp = jnp.repeat(ssd_d[i], p).reshape(1, d)
        ya = _ssd_sweep(order_b, xs_g, bt_g, c_g, cumt, rdt, wt, rev=True, heads=heads, d=d,
                        extra=(y_f, proj, dst["za"] // d, dexp, ssd_norm_g[i].reshape(1, d)))

        yb = _conformer(proj, cf_conv_w[i], cf_conv_b[i], cf_ln_g[i], cf_ln_b[i],
                        row_off=0, rows=l, seg=GRID_W, glu_off=dst["glu"], zb_off=dst["zb"])
        yc = _gate_mul(_fourier_x(proj, tables, l=l, c=fw, v_off=dst["v"]), proj, dst["zc"])
        merged = _merge(ya, yb, yc, proj, w_br, i, rows=l, ya_row_off=0, proj_row_off=0,
                        gate_off=dst["gates"])
        xs2 = _out_proj(merged, w_out, xs2, mod_all, i, 0)
        if update_ctx:
            yb_c = _conformer(proj, cf_conv_w[i], cf_conv_b[i], cf_ln_g[i], cf_ln_b[i],
                              row_off=l, rows=lc, seg=lc, glu_off=dst["glu"], zb_off=dst["zb"])
            yc_c = _fourier_ctx(proj, row_off=l, lc=lc, c=fw, v_off=dst["v"], zc_off=dst["zc"])
            merged_c = _merge(ya, yb_c, yc_c, proj, w_br, i, rows=lc, ya_row_off=l, proj_row_off=l,
                              gate_off=dst["gates"])
            cs2 = _out_proj(merged_c, w_out, cs2, mod_all, i, 1)
    return _final_norm(xs2, final_g)[None]
```

```python
import functools
import math

import numpy as np
import jax
import jax.numpy as jnp
from jax import lax
from jax.experimental import pallas as pl
from jax.experimental.pallas import tpu as pltpu

F32 = jnp.float32
BF16 = jnp.bfloat16

SSD_GROUPS = 8
SSD_STATE = 128
SSD_CHUNK = 128
GRID_W = 64
FN_GROUPS = 8
EPS = 1e-6
FFT_L1 = 64
LOG2E = math.log2(math.e)

SUBLANES = 8
LANES = 128
BF16_SUBLANES = 16
VMEM_LIMIT_BYTES = 56 * 1024 * 1024
HALO = 16


def _params(*sem):
    return pltpu.CompilerParams(dimension_semantics=sem, vmem_limit_bytes=VMEM_LIMIT_BYTES)


def _pick_tile(n, cap, mult):
    best = None
    t = mult
    while t <= min(n, cap):
        if n % t == 0:
            best = t
        t += mult
    assert best is not None, (n, cap, mult)
    return best


def _silu(v):
    return v * jax.nn.sigmoid(v)


def _mod_kernel(s_ref, w_ref, b_ref, o_ref):
    s = _silu(s_ref[...])
    o_ref[...] = jnp.dot(s.astype(BF16), w_ref[...].astype(BF16),
                         preferred_element_type=F32) + b_ref[...]


def _modulation(cond8, w_mod, b_mod):
    depth, d, n3 = w_mod.shape
    tn = _pick_tile(n3, 512, LANES)
    return pl.pallas_call(
        _mod_kernel,
        out_shape=jax.ShapeDtypeStruct((depth, SUBLANES, n3), F32),
        grid=(depth, n3 // tn),
        in_specs=[pl.BlockSpec((SUBLANES, d), lambda l, j: (0, 0)),
                  pl.BlockSpec((None, d, tn), lambda l, j: (l, 0, j)),
                  pl.BlockSpec((None, 1, tn), lambda l, j: (l, 0, j))],
        out_specs=pl.BlockSpec((None, SUBLANES, tn), lambda l, j: (l, 0, j)),
        compiler_params=_params("parallel", "parallel"),
        name="modulation",
    )(cond8, w_mod, b_mod.reshape(depth, 1, n3))


def _norm_mod_kernel(x_ref, c_ref, g_ref, sh_ref, sc_ref, o_ref, *, nbx):
    def emit(v_ref, row):
        v = v_ref[...]
        y = v * lax.rsqrt(jnp.mean(v * v, axis=-1, keepdims=True) + EPS) * g_ref[...]
        o_ref[...] = (y * (1.0 + sc_ref[row:row + 1, :]) + sh_ref[row:row + 1, :]).astype(o_ref.dtype)

    is_x = pl.program_id(0) < nbx
    pl.when(is_x)(lambda: emit(x_ref, 0))
    pl.when(jnp.logical_not(is_x))(lambda: emit(c_ref, 1))


def _norm_mod(x2, c2, g, mod_all, layer):
    l, d = x2.shape
    lc = c2.shape[0]
    tm = _pick_tile(math.gcd(l, lc), 256, BF16_SUBLANES)
    nbx, nbc = l // tm, lc // tm
    return pl.pallas_call(
        functools.partial(_norm_mod_kernel, nbx=nbx),
        out_shape=jax.ShapeDtypeStruct((l + lc, d), BF16),
        grid=(nbx + nbc,),
        in_specs=[pl.BlockSpec((tm, d), lambda i: (jnp.minimum(i, nbx - 1), 0)),
                  pl.BlockSpec((tm, d), lambda i: (jnp.maximum(i - nbx, 0), 0)),
                  pl.BlockSpec((1, d), lambda i: (0, 0)),
                  pl.BlockSpec((None, SUBLANES, d), lambda i: (layer, 0, 0)),
                  pl.BlockSpec((None, SUBLANES, d), lambda i: (layer, 0, 1))],
        out_specs=pl.BlockSpec((tm, d), lambda i: (i, 0)),
        compiler_params=_params("parallel"),
        name="norm_mod",
    )(x2, c2, g.reshape(1, d), mod_all, mod_all)


def _final_norm_kernel(x_ref, g_ref, o_ref):
    x = x_ref[...]
    o_ref[...] = x * lax.rsqrt(jnp.mean(x * x, axis=-1, keepdims=True) + EPS) * g_ref[...]


def _final_norm(x2, g):
    m, d = x2.shape
    tm = _pick_tile(m, 256, SUBLANES)
    return pl.pallas_call(
        _final_norm_kernel,
        out_shape=jax.ShapeDtypeStruct((m, d), F32),
        grid=(m // tm,),
        in_specs=[pl.BlockSpec((tm, d), lambda i: (i, 0)),
                  pl.BlockSpec((1, d), lambda i: (0, 0))],
        out_specs=pl.BlockSpec((tm, d), lambda i: (i, 0)),
        compiler_params=_params("parallel"),
        name="final_norm",
    )(x2, g.reshape(1, d))


def _in_proj_kernel(src_ref, a_ref, w_ref, o_ref, wb_ref):
    del src_ref

    @pl.when(pl.program_id(1) == 0)
    def _():
        wb_ref[...] = w_ref[...].astype(BF16)

    o_ref[...] = jnp.dot(a_ref[...], wb_ref[...], preferred_element_type=F32).astype(o_ref.dtype)


def _in_proj(a, w_in, layer, src_cols, tn, out_dtype, name):
    m, k = a.shape
    nblk = len(src_cols)
    tm = _pick_tile(m, 1408, LANES)
    unit = LANES if all(c % LANES == 0 for c in src_cols) else 1
    src = jnp.asarray([c // unit for c in src_cols], jnp.int32)
    return pl.pallas_call(
        _in_proj_kernel,
        out_shape=jax.ShapeDtypeStruct((m, nblk * tn), out_dtype),
        grid_spec=pltpu.PrefetchScalarGridSpec(
            num_scalar_prefetch=1, grid=(nblk, m // tm),
            in_specs=[pl.BlockSpec((tm, k), lambda j, i, s: (i, 0)),
                      pl.BlockSpec((None, pl.Element(k), pl.Element(tn)),
                                   lambda j, i, s: (layer, 0, s[j] * unit))],
            out_specs=pl.BlockSpec((tm, tn), lambda j, i, s: (i, j)),
            scratch_shapes=[pltpu.VMEM((k, tn), BF16)]),
        compiler_params=_params("parallel", "arbitrary"),
        name=name,
    )(src, a, w_in)


def _out_proj_kernel(a_ref, w_ref, r_ref, g_ref, o_ref, wb_ref, *, row):
    @pl.when(pl.program_id(1) == 0)
    def _():
        wb_ref[...] = w_ref[...].astype(BF16)

    acc = jnp.dot(a_ref[...], wb_ref[...], preferred_element_type=F32)
    o_ref[...] = r_ref[...] + g_ref[row:row + 1, :] * acc


def _out_proj(merged, w_out, resid, mod_all, layer, row):
    m, d = resid.shape
    tm = _pick_tile(m, 1024, LANES)
    tn = _pick_tile(d, 512, LANES)
    goff = 2 * d // tn
    return pl.pallas_call(
        functools.partial(_out_proj_kernel, row=row),
        out_shape=jax.ShapeDtypeStruct((m, d), F32),
        grid=(d // tn, m // tm),
        in_specs=[pl.BlockSpec((tm, d), lambda j, i: (i, 0)),
                  pl.BlockSpec((None, d, tn), lambda j, i: (layer, 0, j)),
                  pl.BlockSpec((tm, tn), lambda j, i: (i, j)),
                  pl.BlockSpec((None, SUBLANES, tn), lambda j, i: (layer, 0, j + goff))],
        out_specs=pl.BlockSpec((tm, tn), lambda j, i: (i, j)),
        scratch_shapes=[pltpu.VMEM((d, tn), BF16)],
        compiler_params=_params("parallel", "arbitrary"),
        name="out_proj",
    )(merged, w_out, resid, mod_all)


def _split3(a):
    hi = a.astype(BF16)
    r1 = a - hi.astype(F32)
    mid = r1.astype(BF16)
    lo = (r1 - mid.astype(F32)).astype(BF16)
    return hi, mid, lo


def _ssd_pre_kernel(cur_ref, prev_ref, next_ref, dtraw_ref, cw_ref, cb_ref, dtb_ref, alog_ref,
                    xs_ref, bt_ref, c_ref, cumt_ref, rdt_ref, wt_ref, ext_ref,
                    *, nbx, nb, d, rp, heads):
    q = SSD_CHUNK
    n = SSD_STATE
    i = pl.program_id(0)
    first = jnp.logical_or(i == 0, i == nbx)
    last = jnp.logical_or(i == nbx - 1, i == nb - 1)
    ext_ref[HALO:HALO + q, :] = cur_ref[...].astype(F32)
    ext_ref[0:HALO, :] = prev_ref[...].astype(F32) * jnp.where(first, 0.0, 1.0)
    ext_ref[HALO + q:2 * HALO + q, :] = next_ref[...].astype(F32) * jnp.where(last, 0.0, 1.0)
    taps = cw_ref.shape[0]
    half = taps // 2

    def conv_silu(c0, width):
        acc = jnp.broadcast_to(cb_ref[:, c0:c0 + width], (q, width))
        for k in range(taps):
            acc = acc + ext_ref[pl.ds(HALO - half + k, q), c0:c0 + width] * cw_ref[k:k + 1, c0:c0 + width]
        return _silu(acc)

    for g in range(SSD_GROUPS):
        xs_ref[g] = conv_silu(g * rp, rp).astype(xs_ref.dtype)
    for g in range(SSD_GROUPS):
        bt_ref[g] = conv_silu(d + g * n, n).T.astype(bt_ref.dtype)
    for g in range(SSD_GROUPS):
        c_ref[g] = conv_silu(d + SSD_GROUPS * n + g * n, n).astype(c_ref.dtype)

    raw = dtraw_ref[...] + dtb_ref[...]
    dt = jnp.maximum(raw, 0.0) + jnp.log1p(jnp.exp(-jnp.abs(raw)))
    a = dt * (-jnp.exp(alog_ref[...]))
    row = lax.broadcasted_iota(jnp.int32, (q, q), 0)
    col = lax.broadcasted_iota(jnp.int32, (q, q), 1)
    tri_f = jnp.where(col <= row, 1.0, 0.0).astype(BF16)
    tri_b = jnp.where(col >= row, 1.0, 0.0).astype(BF16)
    cf = jnp.zeros((q, 2 * heads), F32)
    cb = jnp.zeros((q, 2 * heads), F32)
    for piece in _split3(a):
        cf = cf + jnp.dot(tri_f, piece, preferred_element_type=F32)
        cb = cb + jnp.dot(tri_b, piece, preferred_element_type=F32)
    is_fwd = lax.broadcasted_iota(jnp.int32, (q, 2 * heads), 1) < heads
    cum2 = jnp.where(is_fwd, cf, cb) * LOG2E
    clast = jnp.where(is_fwd[0:1, :], cum2[q - 1:q, :], cum2[0:1, :])
    w = dt * jnp.exp2(clast - cum2)
    cumt_ref[...] = cum2.T
    rdt_ref[...] = (cum2 - jnp.log(dt) * LOG2E).T
    wt_ref[...] = w.T


def _ssd_pre(proj, dt_raw, conv_w, conv_b, dt_bias, a_log, *, nx, d, heads):
    mt = proj.shape[0]
    q, n, g = SSD_CHUNK, SSD_STATE, SSD_GROUPS
    cc = d + 2 * g * n
    rp = d // g
    nb, nbx = mt // q, nx // q
    per = q // HALO
    nhb = mt // HALO
    h2 = 2 * heads
    kern = functools.partial(_ssd_pre_kernel, nbx=nbx, nb=nb, d=d, rp=rp, heads=heads)
    return pl.pallas_call(
        kern,
        out_shape=(jax.ShapeDtypeStruct((g, mt, rp), BF16),
                   jax.ShapeDtypeStruct((g, n, mt), BF16),
                   jax.ShapeDtypeStruct((g, mt, n), BF16),
                   jax.ShapeDtypeStruct((nb, h2, q), F32),
                   jax.ShapeDtypeStruct((nb, h2, q), F32),
                   jax.ShapeDtypeStruct((nb, h2, q), F32)),
        grid=(nb,),
        in_specs=[pl.BlockSpec((q, cc), lambda i: (i, 0)),
                  pl.BlockSpec((HALO, cc), lambda i: (jnp.maximum(i * per - 1, 0), 0)),
                  pl.BlockSpec((HALO, cc), lambda i: (jnp.minimum((i + 1) * per, nhb - 1), 0)),
                  pl.BlockSpec((q, h2), lambda i: (i, 0)),
                  pl.BlockSpec(conv_w.shape, lambda i: (0, 0)),
                  pl.BlockSpec((1, cc), lambda i: (0, 0)),
                  pl.BlockSpec((1, h2), lambda i: (0, 0)),
                  pl.BlockSpec((1, h2), lambda i: (0, 0))],
        out_specs=(pl.BlockSpec((g, q, rp), lambda i: (0, i, 0)),
                   pl.BlockSpec((g, n, q), lambda i: (0, 0, i)),
                   pl.BlockSpec((g, q, n), lambda i: (0, i, 0)),
                   pl.BlockSpec((None, h2, q), lambda i: (i, 0, 0)),
                   pl.BlockSpec((None, h2, q), lambda i: (i, 0, 0)),
                   pl.BlockSpec((None, h2, q), lambda i: (i, 0, 0))),
        scratch_shapes=[pltpu.VMEM((q + 2 * HALO, cc), F32)],
        compiler_params=_params("parallel"),
        name="ssd_pre",
    )(proj, proj, proj, dt_raw, conv_w, conv_b.reshape(1, cc), dt_bias.reshape(1, h2),
      a_log.reshape(1, h2))


def _ssd_sweep_kernel(order_ref, xs_ref, bt_ref, c_ref, cumt_ref, rdt_ref, wt_ref, *rest,
                      rev, heads, hpg, d):
    del order_ref
    q, n, g_cnt = SSD_CHUNK, SSD_STATE, SSD_GROUPS
    if rev:
        yf_ref, za_ref, dexp_ref, ng_ref, out_ref, h_ref, yall_ref = rest
    else:
        out_ref, h_ref = rest
    pw = LANES
    p = pw // 2
    rp = hpg * p
    hoff = heads if rev else 0

    @pl.when(pl.program_id(0) == 0)
    def _():
        h_ref[...] = jnp.zeros_like(h_ref)

    row = lax.broadcasted_iota(jnp.int32, (q, q), 0)
    col = lax.broadcasted_iota(jnp.int32, (q, q), 1)
    tri = (col >= row) if rev else (col <= row)
    left = lax.broadcasted_iota(jnp.int32, (q, pw), 1) < p
    left1 = left[0:1, :]
    edge = 0 if rev else q - 1

    def group_body(g, carry):
        xs_g = xs_ref[g]
        c_g = c_ref[g]
        bt_g = bt_ref[g]
        s_mat = jnp.dot(c_g, bt_g, preferred_element_type=F32)
        c_f = c_g.astype(F32)
        bt_f = bt_g.astype(F32)
        for pair in range(hpg // 2):
            sl = slice(pair * pw, (pair + 1) * pw)
            xs2 = xs_g[:, sl]
            zero_x = jnp.zeros_like(xs2)
            xs_l = jnp.where(left, xs2, zero_x)
            xs_r = jnp.where(left, zero_x, xs2)
            h2 = h_ref[g, :, sl]
            h2b = h2.astype(BF16)
            zero_h = jnp.zeros_like(h2b)
            hs_l = jnp.where(left, h2b, zero_h)
            hs_r = jnp.where(left, zero_h, h2b)
            w_l, cs_l, bts_l, cd_l = [], [], [], []
            for e in range(2):
                hidx = hoff + g * hpg + 2 * pair + e
                colb = jnp.broadcast_to(cumt_ref[pl.ds(hidx, 1), :], (q, q)).T
                rowq = rdt_ref[pl.ds(hidx, 1), :]
                roww = wt_ref[pl.ds(hidx, 1), :]
                dec = jnp.exp2(jnp.where(tri, colb - rowq, -1e30))
                w_l.append((s_mat * dec).astype(BF16))
                ecol = jnp.exp2(colb)
                cs_l.append((c_f * ecol).astype(BF16))
                bts_l.append((bt_f * roww).astype(BF16))
                cd_l.append(ecol[edge:edge + 1, :])
            lhs_y = jnp.concatenate(w_l + cs_l, axis=1)
            rhs_y = jnp.concatenate([xs_l, xs_r, hs_l, hs_r], axis=0)
            y2 = jnp.dot(lhs_y, rhs_y, preferred_element_type=F32)
            st2 = jnp.dot(jnp.concatenate(bts_l, axis=1), jnp.concatenate([xs_l, xs_r], axis=0),
                          preferred_element_type=F32)
            cd2 = jnp.where(left1, cd_l[0], cd_l[1])
            h_ref[g, :, sl] = h2 * cd2 + st2
            if rev:
                yall_ref[g, :, sl] = y2 + yf_ref[g, :, sl]
            else:
                out_ref[g, :, sl] = y2
        return carry

    lax.fori_loop(0, g_cnt, group_body, 0, unroll=4)

    if rev:
        ssq = jnp.zeros((q, 1), F32)
        for g in range(g_cnt):
            gs = slice(g * rp, (g + 1) * rp)
            y = yall_ref[g] + dexp_ref[:, gs] * xs_ref[g].astype(F32)
            t = y * _silu(za_ref[:, gs].astype(F32))
            yall_ref[g] = t
            ssq = ssq + jnp.sum(t * t, axis=1, keepdims=True)
        scale = lax.rsqrt(ssq / d + EPS)
        for g in range(g_cnt):
            gs = slice(g * rp, (g + 1) * rp)
            out_ref[:, gs] = (yall_ref[g] * scale * ng_ref[:, gs]).astype(out_ref.dtype)


def _ssd_sweep(order, xs, bt, cm, cumt, rdt, wt, *, rev, heads, d, extra=None):
    g, mt, rp = xs.shape
    q, n = SSD_CHUNK, SSD_STATE
    nb = mt // q
    h2 = 2 * heads
    hpg = heads // g
    assert hpg % 2 == 0 and rp == hpg * (LANES // 2)
    in_specs = [pl.BlockSpec((g, q, rp), lambda s, o: (0, o[s], 0)),
                pl.BlockSpec((g, n, q), lambda s, o: (0, 0, o[s])),
                pl.BlockSpec((g, q, n), lambda s, o: (0, o[s], 0)),
                pl.BlockSpec((None, h2, q), lambda s, o: (o[s], 0, 0)),
                pl.BlockSpec((None, h2, q), lambda s, o: (o[s], 0, 0)),
                pl.BlockSpec((None, h2, q), lambda s, o: (o[s], 0, 0))]
    args = [xs, bt, cm, cumt, rdt, wt]
    scratch = [pltpu.VMEM((g, n, rp), F32)]
    if rev:
        yf, proj, za_blk, dexp, ng = extra
        in_specs += [pl.BlockSpec((g, q, rp), lambda s, o: (0, o[s], 0)),
                     pl.BlockSpec((q, d), lambda s, o: (o[s], za_blk)),
                     pl.BlockSpec((1, d), lambda s, o: (0, 0)),
                     pl.BlockSpec((1, d), lambda s, o: (0, 0))]
        args += [yf, proj, dexp, ng]
        out_shape = jax.ShapeDtypeStruct((mt, d), BF16)
        out_spec = pl.BlockSpec((q, d), lambda s, o: (o[s], 0))
        scratch.append(pltpu.VMEM((g, q, rp), F32))
    else:
        out_shape = jax.ShapeDtypeStruct((g, mt, rp), F32)
        out_spec = pl.BlockSpec((g, q, rp), lambda s, o: (0, o[s], 0))
    kern = functools.partial(_ssd_sweep_kernel, rev=rev, heads=heads, hpg=hpg, d=d)
    return pl.pallas_call(
        kern,
        out_shape=out_shape,
        grid_spec=pltpu.PrefetchScalarGridSpec(
            num_scalar_prefetch=1, grid=(nb,), in_specs=in_specs, out_specs=out_spec,
            scratch_shapes=scratch),
        compiler_params=_params("arbitrary"),
        name="ssd_sweep_bwd" if rev else "ssd_sweep_fwd",
    )(order, *args)


def _conformer_kernel(glu_ref, zb_ref, cw_ref, cb_ref, lg_ref, lb_ref, o_ref, pad_ref, v_ref,
                      *, seg, nseg, cf):
    taps = cw_ref.shape[0]
    half = taps // 2
    rc = min(seg, 64)
    ct = 2 * LANES
    u = glu_ref[:, 0:cf].astype(F32) * jax.nn.sigmoid(glu_ref[:, cf:2 * cf].astype(F32))
    zlo = jnp.zeros((HALO, cf), F32)
    zhi = jnp.zeros((HALO + SUBLANES, cf), F32)
    for ph in range(SUBLANES):
        for s in range(nseg):
            pad_ref[ph, s, 0:HALO, :] = zlo
            pad_ref[ph, s, seg + HALO - SUBLANES:seg + 2 * HALO, :] = zhi
            pad_ref[ph, s, HALO - ph:HALO - ph + seg, :] = u[s * seg:(s + 1) * seg, :]
    for c0 in range(0, cf, ct):
        for s in range(nseg):
            for r0 in range(0, seg, rc):
                acc = jnp.broadcast_to(cb_ref[:, c0:c0 + ct], (rc, ct))
                for k in range(taps):
                    off = HALO + k - half
                    ph = off % SUBLANES
                    base = off - ph + r0
                    acc = acc + pad_ref[ph, s, base:base + rc, c0:c0 + ct] * cw_ref[k:k + 1, c0:c0 + ct]
                v_ref[s * seg + r0:s * seg + r0 + rc, c0:c0 + ct] = acc
    v = v_ref[...]
    mu = jnp.mean(v, axis=-1, keepdims=True)
    xc = v - mu
    y = xc * lax.rsqrt(jnp.mean(xc * xc, axis=-1, keepdims=True) + EPS) * lg_ref[...] + lb_ref[...]
    o_ref[...] = (_silu(y) * _silu(zb_ref[...].astype(F32))).astype(o_ref.dtype)


def _conformer(proj, conv_w, conv_b, ln_g, ln_b, *, row_off, rows, seg, glu_off, zb_off):
    taps, cf = conv_w.shape
    assert taps // 2 <= HALO
    tb = seg if seg >= 128 else 128
    nseg = tb // seg
    roff = row_off // tb
    kern = functools.partial(_conformer_kernel, seg=seg, nseg=nseg, cf=cf)
    return pl.pallas_call(
        kern,
        out_shape=jax.ShapeDtypeStruct((rows, cf), BF16),
        grid=(rows // tb,),
        in_specs=[pl.BlockSpec((tb, 2 * cf), lambda i: (i + roff, glu_off // (2 * cf))),
                  pl.BlockSpec((tb, cf), lambda i: (i + roff, zb_off // cf)),
                  pl.BlockSpec((taps, cf), lambda i: (0, 0)),
                  pl.BlockSpec((1, cf), lambda i: (0, 0)),
                  pl.BlockSpec((1, cf), lambda i: (0, 0)),
                  pl.BlockSpec((1, cf), lambda i: (0, 0))],
        out_specs=pl.BlockSpec((tb, cf), lambda i: (i, 0)),
        scratch_shapes=[pltpu.VMEM((SUBLANES, nseg, seg + 2 * HALO, cf), F32),
                        pltpu.VMEM((tb, cf), F32)],
        compiler_params=_params("parallel"),
        name="conformer",
    )(proj, proj, conv_w, conv_b.reshape(1, cf), ln_g.reshape(1, cf), ln_b.reshape(1, cf))


def _dft_cs(n):
    idx = np.arange(n)
    ang = 2.0 * np.pi * ((idx[:, None] * idx[None, :]) % n) / n
    return np.cos(ang), np.sin(ang)


def _fft_tables(l, gd):
    l1 = FFT_L1
    l2 = l // l1
    s = SUBLANES
    eye = np.eye(s)
    c1, s1 = _dft_cs(l1)
    c2, s2 = _dft_cs(l2)

    def kron_a(m):
        m4 = m.reshape(l1 // s, 1, s, l1, 1)
        return (m4 * eye.reshape(1, s, 1, 1, s)).reshape(l1 * s, l1 * s)

    def const_bf16(m):
        return jnp.asarray(m.astype(np.float32)).astype(BF16)

    ka = const_bf16(np.concatenate([kron_a(c1), kron_a(s1)], axis=0))
    mb = const_bf16(np.block([[c2, -s2], [s2, c2]]))
    k1 = np.arange(l1)[:, None]
    t2 = np.arange(l2)[None, :]
    ang = 2.0 * np.pi * ((k1 * t2) % l) / l

    def tw(m):
        m4 = m.astype(np.float32).reshape(l1 // s, s, l2, 1)
        return jnp.asarray(np.ascontiguousarray(np.broadcast_to(m4, (l1 // s, s, l2, LANES))))

    return ka, mb, tw(np.cos(ang)), tw(np.sin(ang)), _chan_table(l, gd)


def _chan_table(l, gd):
    cc, sc = _dft_cs(gd)
    return jnp.asarray(np.concatenate([cc, -sc], axis=0) / math.sqrt(l * gd), F32).astype(BF16)


def _fft_a_kernel(v_ref, ka_ref, or_ref, oi_ref):
    l1, rows, cb = v_ref.shape
    s = SUBLANES
    nkb = or_ref.shape[0]
    xf = v_ref[...].astype(F32)
    for h in range(rows // s):
        xb = xf[:, h * s:(h + 1) * s, :].reshape(l1 * s, cb).astype(BF16)
        y = jnp.dot(ka_ref[...], xb, preferred_element_type=F32)
        or_ref[:, h * s:(h + 1) * s, :, :] = y[0:l1 * s].reshape(nkb, s, s, cb)
        oi_ref[:, h * s:(h + 1) * s, :, :] = y[l1 * s:2 * l1 * s].reshape(nkb, s, s, cb)


def _fft_b_kernel(*refs, nh):
    ar_refs, ai_refs = refs[0:nh], refs[nh:2 * nh]
    twc_ref, tws_ref, mb_ref, tc_ref, o_ref = refs[2 * nh:2 * nh + 5]
    f_refs = refs[2 * nh + 5:]
    l2, s, _ = o_ref.shape
    for r in range(s):
        c, sn = twc_ref[r], tws_ref[r]
        br, bi = [], []
        for h in range(nh):
            ar = ar_refs[h][pl.ds(r, l2, stride=s), :]
            ai = ai_refs[h][pl.ds(r, l2, stride=s), :]
            br.append(ar * c - ai * sn)
            bi.append(ar * sn + ai * c)
        stack = jnp.concatenate([jnp.concatenate(br, axis=1), jnp.concatenate(bi, axis=1)], axis=0)
        z = jnp.dot(mb_ref[...], stack.astype(BF16), preferred_element_type=F32)
        xri = jnp.concatenate([z[0:l2], z[l2:2 * l2]], axis=1).astype(BF16)
        f = jnp.dot(xri, tc_ref[...], preferred_element_type=F32)
        for h in range(nh):
            f_refs[h][pl.ds(r, l2, stride=s), :] = f[:, h * LANES:(h + 1) * LANES]
    for h in range(nh):
        o_ref[:, :, h * LANES:(h + 1) * LANES] = f_refs[h][...].reshape(l2, s, LANES)


def _fourier_x(proj, tables, *, l, c, v_off):
    ka, mb, twc, tws, tc = tables
    l1, s = FFT_L1, SUBLANES
    l2 = l // l1
    gd = c // FN_GROUPS
    nkb = l1 // s
    mt, ncols = proj.shape
    tr = BF16_SUBLANES
    assert mt % l2 == 0 and l2 % tr == 0
    ap_shape = jax.ShapeDtypeStruct((nkb, l2, s, c), F32)
    voff = v_off // gd
    apr, api = pl.pallas_call(
        _fft_a_kernel,
        out_shape=(ap_shape, ap_shape),
        grid=(l2 // tr, c // gd),
        in_specs=[pl.BlockSpec((l1, tr, gd), lambda i, j: (0, i, voff + j)),
                  pl.BlockSpec(ka.shape, lambda i, j: (0, 0))],
        out_specs=(pl.BlockSpec((nkb, tr, s, gd), lambda i, j: (0, i, 0, j)),
                   pl.BlockSpec((nkb, tr, s, gd), lambda i, j: (0, i, 0, j))),
        compiler_params=_params("parallel", "parallel"),
        name="fft_stage_a",
    )(proj.reshape(mt // l2, l2, ncols), ka)
    nh = gd // LANES
    slab_specs = [pl.BlockSpec((None, l2 * s, LANES), lambda kq, j, h=h: (kq, 0, j * nh + h))
                  for h in range(nh)]
    apr2, api2 = apr.reshape(nkb, l2 * s, c), api.reshape(nkb, l2 * s, c)
    f = pl.pallas_call(
        functools.partial(_fft_b_kernel, nh=nh),
        out_shape=jax.ShapeDtypeStruct((l2, nkb, s, c), F32),
        grid=(nkb, c // gd),
        in_specs=slab_specs + slab_specs + [
            pl.BlockSpec((None, s, l2, LANES), lambda kq, j: (kq, 0, 0, 0)),
            pl.BlockSpec((None, s, l2, LANES), lambda kq, j: (kq, 0, 0, 0)),
            pl.BlockSpec(mb.shape, lambda kq, j: (0, 0)),
            pl.BlockSpec(tc.shape, lambda kq, j: (0, 0))],
        out_specs=pl.BlockSpec((l2, None, s, gd), lambda kq, j: (0, kq, 0, j)),
        scratch_shapes=[pltpu.VMEM((l2 * s, LANES), F32)] * nh,
        compiler_params=_params("parallel", "parallel"),
        name="fft_stage_b",
    )(*([apr2] * nh + [api2] * nh), twc, tws, mb, tc)
    return f.reshape(l, c)


def _gate_mul_kernel(f_ref, z_ref, o_ref):
    o_ref[...] = (f_ref[...] * _silu(z_ref[...].astype(F32))).astype(o_ref.dtype)


def _gate_mul(f, proj, zc_off):
    l, c = f.shape
    tm = _pick_tile(l, 512, BF16_SUBLANES)
    return pl.pallas_call(
        _gate_mul_kernel,
        out_shape=jax.ShapeDtypeStruct((l, c), BF16),
        grid=(l // tm,),
        in_specs=[pl.BlockSpec((tm, c), lambda i: (i, 0)),
                  pl.BlockSpec((tm, c), lambda i: (i, zc_off // c))],
        out_specs=pl.BlockSpec((tm, c), lambda i: (i, 0)),
        compiler_params=_params("parallel"),
        name="fourier_gate",
    )(f, proj)


def _fft_ctx_kernel(v_ref, z_ref, cs_ref, tc_ref, o_ref):
    lc = v_ref.shape[0]
    p = jnp.dot(cs_ref[...], v_ref[...], preferred_element_type=F32)
    pri = jnp.concatenate([p[0:lc], p[lc:2 * lc]], axis=1).astype(BF16)
    f = jnp.dot(pri, tc_ref[...], preferred_element_type=F32)
    o_ref[...] = (f * _silu(z_ref[...].astype(F32))).astype(o_ref.dtype)


def _fourier_ctx(proj, *, row_off, lc, c, v_off, zc_off):
    gd = c // FN_GROUPS
    cc, sc = _dft_cs(lc)
    cs = jnp.asarray(np.concatenate([cc, sc], axis=0), F32).astype(BF16)
    tc = _chan_table(lc, gd)
    rblk = row_off // lc
    return pl.pallas_call(
        _fft_ctx_kernel,
        out_shape=jax.ShapeDtypeStruct((lc, c), BF16),
        grid=(FN_GROUPS,),
        in_specs=[pl.BlockSpec((lc, gd), lambda g: (rblk, v_off // gd + g)),
                  pl.BlockSpec((lc, gd), lambda g: (rblk, zc_off // gd + g)),
                  pl.BlockSpec(cs.shape, lambda g: (0, 0)),
                  pl.BlockSpec(tc.shape, lambda g: (0, 0))],
        out_specs=pl.BlockSpec((lc, gd), lambda g: (0, g)),
        compiler_params=_params("parallel"),
        name="fourier_ctx",
    )(proj, proj, cs, tc)


def _merge_kernel(ya_ref, yb_ref, yc_ref, ga_ref, gb_ref, gc_ref, wa_ref, wb_ref, wc_ref, o_ref):
    def term(y_ref, w_ref, g_ref):
        return jax.nn.sigmoid(g_ref[...].astype(F32)) * jnp.dot(
            y_ref[...], w_ref[...], preferred_element_type=F32)
    acc = term(ya_ref, wa_ref, ga_ref) + term(yb_ref, wb_ref, gb_ref) + term(yc_ref, wc_ref, gc_ref)
    o_ref[...] = acc.astype(o_ref.dtype)


def _merge(ya, yb, yc, proj, w_branch, layer, *, rows, ya_row_off, proj_row_off, gate_off):
    d = ya.shape[1]
    cf, fw = yb.shape[1], yc.shape[1]
    tm = _pick_tile(rows, 1024, LANES)
    tn = _pick_tile(d, 256, LANES)
    ra, rpj = ya_row_off // tm, proj_row_off // tm
    gblk = gate_off // tn
    dblk = d // tn
    return pl.pallas_call(
        _merge_kernel,
        out_shape=jax.ShapeDtypeStruct((rows, d), BF16),
        grid=(rows // tm, d // tn),
        in_specs=[pl.BlockSpec((tm, d), lambda i, j: (i + ra, 0)),
                  pl.BlockSpec((tm, cf), lambda i, j: (i, 0)),
                  pl.BlockSpec((tm, fw), lambda i, j: (i, 0)),
                  pl.BlockSpec((tm, tn), lambda i, j: (i + rpj, gblk + j)),
                  pl.BlockSpec((tm, tn), lambda i, j: (i + rpj, gblk + dblk + j)),
                  pl.BlockSpec((tm, tn), lambda i, j: (i + rpj, gblk + 2 * dblk + j)),
                  pl.BlockSpec((None, d, tn), lambda i, j: (layer, 0, j)),
                  pl.BlockSpec((None, cf, tn), lambda i, j: (layer, d // cf, j)),
                  pl.BlockSpec((None, fw, tn), lambda i, j: (layer, (d + cf) // fw, j))],
        out_specs=pl.BlockSpec((tm, tn), lambda i, j: (i, j)),
        compiler_params=_params("parallel", "parallel"),
        name="merge",
    )(ya, yb, yc, proj, proj, proj, w_branch, w_branch, w_branch)


def _column_layout(d, cf, fw, heads):
    gn = SSD_GROUPS * SSD_STATE
    cc = d + 2 * gn
    src = {}
    pos = 0
    for name, width in (("xbc", cc), ("za", d), ("dt", 2 * heads), ("glu", 2 * cf), ("zb", cf),
                        ("v", fw), ("zc", fw), ("gates", 3 * d)):
        src[name] = (pos, width)
        pos += width
    order = (("xbc", cc), ("zb", cf), ("v", fw), ("zc", fw), ("za", d), ("glu", 2 * cf),
             ("gates", d), ("dt", 2 * heads))
    dst = {}
    pieces = []
    pos = 0
    for name, align in order:
        padw = (-pos) % align
        if padw:
            pieces.append(("pad", padw))
            pos += padw
        dst[name] = pos
        pieces.append((name, src[name]))
        pos += src[name][1]
    return dst, pieces, pos


def _source_columns(pieces):
    main = [(name, spec) for name, spec in pieces if name != "dt"]
    widths = [spec if name == "pad" else spec[1] for name, spec in main]
    tn = _pick_tile(functools.reduce(math.gcd, widths), 512, LANES)
    cols = []
    for name, spec in main:
        start, width = (0, spec) if name == "pad" else spec
        cols += [start + k * tn if name != "pad" else 0 for k in range(width // tn)]
    return tn, cols


def kernel(x, c, ctx, c_ctx, w_mod, b_mod, norm_g, w_in, ssd_conv_w, ssd_conv_b, ssd_dt_bias,
           ssd_a_log, ssd_d, ssd_norm_g, cf_conv_w, cf_conv_b, cf_ln_g, cf_ln_b, w_branch,
           w_out, final_g):
    bsz, l, d = x.shape
    assert bsz == 1, "single-sample kernel"
    lc = ctx.shape[1]
    depth = w_mod.shape[0]
    heads = ssd_dt_bias.shape[-1]
    cf = cf_conv_w.shape[-1]
    fw = w_branch.shape[1] - d - cf
    p = d // heads
    assert 2 * p == LANES and l % SSD_CHUNK == 0 and lc % SSD_CHUNK == 0 and l % lc == 0
    mt = l + lc
    nbx, nb = l // SSD_CHUNK, mt // SSD_CHUNK

    dst, pieces, _ = _column_layout(d, cf, fw, heads)
    tn_in, src_cols = _source_columns(pieces)
    dt_src = [spec[0] for name, spec in pieces if name == "dt"]

    cond8 = jnp.zeros((SUBLANES, d), F32).at[0].set(c[0]).at[1].set(c_ctx)
    mod_all = _modulation(cond8, w_mod, b_mod)

    order_f = jnp.asarray(list(range(nbx, nb)) + list(range(nbx)), jnp.int32)
    order_b = jnp.asarray(list(range(nb - 1, nbx - 1, -1)) + list(range(nbx - 1, -1, -1)), jnp.int32)
    tables = _fft_tables(l, fw // FN_GROUPS)

    xs2 = x[0]
    cs2 = ctx[0]
    w_br = w_branch.astype(BF16)
    for i in range(depth):
        update_ctx = i < depth - 1
        hcat = _norm_mod(xs2, cs2, norm_g[i], mod_all, i)
        proj = _in_proj(hcat, w_in, i, src_cols, tn_in, BF16, "in_proj")
        dt_raw = _in_proj(hcat, w_in, i, dt_src, 2 * heads, F32, "in_proj_dt")

        xs_g, bt_g, c_g, cumt, rdt, wt = _ssd_pre(
            proj, dt_raw, ssd_conv_w[i], ssd_conv_b[i], ssd_dt_bias[i].reshape(-1),
            ssd_a_log[i].reshape(-1), nx=l, d=d, heads=heads)
        y_f = _ssd_sweep(order_f, xs_g, bt_g, c_g, cumt, rdt, wt, rev=False, heads=heads, d=d)
        dexp = jnp.repeat(ssd_d[i], p).reshape(1, d)
        ya = _ssd_sweep(order_b, xs_g, bt_g, c_g, cumt, rdt, wt, rev=True, heads=heads, d=d,
                        extra=(y_f, proj, dst["za"] // d, dexp, ssd_norm_g[i].reshape(1, d)))

        yb = _conformer(proj, cf_conv_w[i], cf_conv_b[i], cf_ln_g[i], cf_ln_b[i],
                        row_off=0, rows=l, seg=GRID_W, glu_off=dst["glu"], zb_off=dst["zb"])
        yc = _gate_mul(_fourier_x(proj, tables, l=l, c=fw, v_off=dst["v"]), proj, dst["zc"])
        merged = _merge(ya, yb, yc, proj, w_br, i, rows=l, ya_row_off=0, proj_row_off=0,
                        gate_off=dst["gates"])
        xs2 = _out_proj(merged, w_out, xs2, mod_all, i, 0)
        if update_ctx:
            yb_c = _conformer(proj, cf_conv_w[i], cf_conv_b[i], cf_ln_g[i], cf_ln_b[i],
                              row_off=l, rows=lc, seg=lc, glu_off=dst["glu"], zb_off=dst["zb"])
            yc_c = _fourier_ctx(proj, row_off=l, lc=lc, c=fw, v_off=dst["v"], zc_off=dst["zc"])
            merged_c = _merge(ya, yb_c, yc_c, proj, w_br, i, rows=lc, ya_row_off=l, proj_row_off=l,
                              gate_off=dst["gates"])
            cs2 = _out_proj(merged_c, w_out, cs2, mod_all, i, 1)
    return _final_norm(xs2, final_g)[None]
```

```python
import functools
import math

import numpy as np
import jax
import jax.numpy as jnp
from jax import lax
from jax.experimental import pallas as pl
from jax.experimental.pallas import tpu as pltpu

F32 = jnp.float32
BF16 = jnp.bfloat16

SSD_GROUPS = 8
SSD_STATE = 128
SSD_CHUNK = 128
GRID_W = 64
FN_GROUPS = 8
EPS = 1e-6
FFT_L1 = 64
LOG2E = math.log2(math.e)

SUBLANES = 8
LANES = 128
BF16_SUBLANES = 16
VMEM_LIMIT_BYTES = 56 * 1024 * 1024
HALO = 16


def _params(*sem):
    return pltpu.CompilerParams(dimension_semantics=sem, vmem_limit_bytes=VMEM_LIMIT_BYTES)


def _pick_tile(n, cap, mult):
    best = None
    t = mult
    while t <= min(n, cap):
        if n % t == 0:
            best = t
        t += mult
    assert best is not None, (n, cap, mult)
    return best


def _silu(v):
    return v * jax.nn.sigmoid(v)


def _mod_kernel(s_ref, w_ref, b_ref, o_ref):
    s = _silu(s_ref[...])
    o_ref[...] = jnp.dot(s.astype(BF16), w_ref[...].astype(BF16),
                         preferred_element_type=F32) + b_ref[...]


def _modulation(cond8, w_mod, b_mod):
    depth, d, n3 = w_mod.shape
    tn = _pick_tile(n3, 512, LANES)
    return pl.pallas_call(
        _mod_kernel,
        out_shape=jax.ShapeDtypeStruct((depth, SUBLANES, n3), F32),
        grid=(depth, n3 // tn),
        in_specs=[pl.BlockSpec((SUBLANES, d), lambda l, j: (0, 0)),
                  pl.BlockSpec((None, d, tn), lambda l, j: (l, 0, j)),
                  pl.BlockSpec((None, 1, tn), lambda l, j: (l, 0, j))],
        out_specs=pl.BlockSpec((None, SUBLANES, tn), lambda l, j: (l, 0, j)),
        compiler_params=_params("parallel", "parallel"),
        name="modulation",
    )(cond8, w_mod, b_mod.reshape(depth, 1, n3))


def _norm_mod_kernel(x_ref, c_ref, g_ref, sh_ref, sc_ref, o_ref, *, nbx):
    def emit(v_ref, row):
        v = v_ref[...]
        y = v * lax.rsqrt(jnp.mean(v * v, axis=-1, keepdims=True) + EPS) * g_ref[...]
        o_ref[...] = (y * (1.0 + sc_ref[row:row + 1, :]) + sh_ref[row:row + 1, :]).astype(o_ref.dtype)

    is_x = pl.program_id(0) < nbx
    pl.when(is_x)(lambda: emit(x_ref, 0))
    pl.when(jnp.logical_not(is_x))(lambda: emit(c_ref, 1))


def _norm_mod(x2, c2, g, mod_all, layer):
    l, d = x2.shape
    lc = c2.shape[0]
    tm = _pick_tile(math.gcd(l, lc), 256, BF16_SUBLANES)
    nbx, nbc = l // tm, lc // tm
    return pl.pallas_call(
        functools.partial(_norm_mod_kernel, nbx=nbx),
        out_shape=jax.ShapeDtypeStruct((l + lc, d), BF16),
        grid=(nbx + nbc,),
        in_specs=[pl.BlockSpec((tm, d), lambda i: (jnp.minimum(i, nbx - 1), 0)),
                  pl.BlockSpec((tm, d), lambda i: (jnp.maximum(i - nbx, 0), 0)),
                  pl.BlockSpec((1, d), lambda i: (0, 0)),
                  pl.BlockSpec((None, SUBLANES, d), lambda i: (layer, 0, 0)),
                  pl.BlockSpec((None, SUBLANES, d), lambda i: (layer, 0, 1))],
        out_specs=pl.BlockSpec((tm, d), lambda i: (i, 0)),
        compiler_params=_params("parallel"),
        name="norm_mod",
    )(x2, c2, g.reshape(1, d), mod_all, mod_all)


def _final_norm_kernel(x_ref, g_ref, o_ref):
    x = x_ref[...]
    o_ref[...] = x * lax.rsqrt(jnp.mean(x * x, axis=-1, keepdims=True) + EPS) * g_ref[...]


def _final_norm(x2, g):
    m, d = x2.shape
    tm = _pick_tile(m, 256, SUBLANES)
    return pl.pallas_call(
        _final_norm_kernel,
        out_shape=jax.ShapeDtypeStruct((m, d), F32),
        grid=(m // tm,),
        in_specs=[pl.BlockSpec((tm, d), lambda i: (i, 0)),
                  pl.BlockSpec((1, d), lambda i: (0, 0))],
        out_specs=pl.BlockSpec((tm, d), lambda i: (i, 0)),
        compiler_params=_params("parallel"),
        name="final_norm",
    )(x2, g.reshape(1, d))


def _in_proj_kernel(src_ref, a_ref, w_ref, o_ref, wb_ref):
    del src_ref

    @pl.when(pl.program_id(1) == 0)
    def _():
        wb_ref[...] = w_ref[...].astype(BF16)

    o_ref[...] = jnp.dot(a_ref[...], wb_ref[...], preferred_element_type=F32).astype(o_ref.dtype)


def _in_proj(a, w_in, layer, src_cols, tn, out_dtype, name):
    m, k = a.shape
    nblk = len(src_cols)
    tm = _pick_tile(m, 1408, LANES)
    unit = LANES if all(c % LANES == 0 for c in src_cols) else 1
    src = jnp.asarray([c // unit for c in src_cols], jnp.int32)
    return pl.pallas_call(
        _in_proj_kernel,
        out_shape=jax.ShapeDtypeStruct((m, nblk * tn), out_dtype),
        grid_spec=pltpu.PrefetchScalarGridSpec(
            num_scalar_prefetch=1, grid=(nblk, m // tm),
            in_specs=[pl.BlockSpec((tm, k), lambda j, i, s: (i, 0)),
                      pl.BlockSpec((None, pl.Element(k), pl.Element(tn)),
                                   lambda j, i, s: (layer, 0, s[j] * unit))],
            out_specs=pl.BlockSpec((tm, tn), lambda j, i, s: (i, j)),
            scratch_shapes=[pltpu.VMEM((k, tn), BF16)]),
        compiler_params=_params("parallel", "arbitrary"),
        name=name,
    )(src, a, w_in)


def _out_proj_kernel(a_ref, w_ref, r_ref, g_ref, o_ref, wb_ref, *, row):
    @pl.when(pl.program_id(1) == 0)
    def _():
        wb_ref[...] = w_ref[...].astype(BF16)

    acc = jnp.dot(a_ref[...], wb_ref[...], preferred_element_type=F32)
    o_ref[...] = r_ref[...] + g_ref[row:row + 1, :] * acc


def _out_proj(merged, w_out, resid, mod_all, layer, row):
    m, d = resid.shape
    tm = _pick_tile(m, 1024, LANES)
    tn = _pick_tile(d, 512, LANES)
    goff = 2 * d // tn
    return pl.pallas_call(
        functools.partial(_out_proj_kernel, row=row),
        out_shape=jax.ShapeDtypeStruct((m, d), F32),
        grid=(d // tn, m // tm),
        in_specs=[pl.BlockSpec((tm, d), lambda j, i: (i, 0)),
                  pl.BlockSpec((None, d, tn), lambda j, i: (layer, 0, j)),
                  pl.BlockSpec((tm, tn), lambda j, i: (i, j)),
                  pl.BlockSpec((None, SUBLANES, tn), lambda j, i: (layer, 0, j + goff))],
        out_specs=pl.BlockSpec((tm, tn), lambda j, i: (i, j)),
        scratch_shapes=[pltpu.VMEM((d, tn), BF16)],
        compiler_params=_params("parallel", "arbitrary"),
        name="out_proj",
    )(merged, w_out, resid, mod_all)


def _split3(a):
    hi = a.astype(BF16)
    r1 = a - hi.astype(F32)
    mid = r1.astype(BF16)
    lo = (r1 - mid.astype(F32)).astype(BF16)
    return hi, mid, lo


def _ssd_pre_kernel(cur_ref, prev_ref, next_ref, dtraw_ref, sh_ref, cw_ref, cb_ref, dtb_ref, alog_ref,
                    xs_ref, bt_ref, c_ref, cumt_ref, rdt_ref, wt_ref,
                    *, nbx, nb, d, rp, heads):
    q = SSD_CHUNK
    n = SSD_STATE
    i = pl.program_id(0)
    first = jnp.logical_or(i == 0, i == nbx)
    last = jnp.logical_or(i == nbx - 1, i == nb - 1)
    pmask = jnp.where(first, 0.0, 1.0).astype(BF16)
    nmask = jnp.where(last, 0.0, 1.0).astype(BF16)
    taps = cw_ref.shape[0]

    def conv_silu(c0, width):
        cols = slice(c0, c0 + width)
        ext = jnp.concatenate([prev_ref[:, cols] * pmask, cur_ref[:, cols], next_ref[:, cols] * nmask],
                              axis=0)
        win = jnp.dot(sh_ref[...], ext, preferred_element_type=F32)
        acc = cb_ref[:, cols] + cur_ref[:, cols].astype(F32) * cw_ref[taps // 2:taps // 2 + 1, cols]
        for j, k in enumerate(kk for kk in range(taps) if kk != taps // 2):
            acc = acc + win[j * q:(j + 1) * q, :] * cw_ref[k:k + 1, cols]
        return _silu(acc)

    for g in range(SSD_GROUPS):
        xs_ref[g] = conv_silu(g * rp, rp).astype(xs_ref.dtype)
    for g in range(SSD_GROUPS):
        bt_ref[g] = conv_silu(d + g * n, n).T.astype(bt_ref.dtype)
    for g in range(SSD_GROUPS):
        c_ref[g] = conv_silu(d + SSD_GROUPS * n + g * n, n).astype(c_ref.dtype)

    raw = dtraw_ref[...] + dtb_ref[...]
    dt = jnp.maximum(raw, 0.0) + jnp.log1p(jnp.exp(-jnp.abs(raw)))
    a = dt * (-jnp.exp(alog_ref[...]))
    row = lax.broadcasted_iota(jnp.int32, (q, q), 0)
    col = lax.broadcasted_iota(jnp.int32, (q, q), 1)
    tri_f = jnp.where(col <= row, 1.0, 0.0).astype(BF16)
    tri_b = jnp.where(col >= row, 1.0, 0.0).astype(BF16)
    cf = jnp.zeros((q, 2 * heads), F32)
    cb = jnp.zeros((q, 2 * heads), F32)
    for piece in _split3(a):
        cf = cf + jnp.dot(tri_f, piece, preferred_element_type=F32)
        cb = cb + jnp.dot(tri_b, piece, preferred_element_type=F32)
    is_fwd = lax.broadcasted_iota(jnp.int32, (q, 2 * heads), 1) < heads
    cum2 = jnp.where(is_fwd, cf, cb) * LOG2E
    clast = jnp.where(is_fwd[0:1, :], cum2[q - 1:q, :], cum2[0:1, :])
    w = dt * jnp.exp2(clast - cum2)
    cumt_ref[...] = cum2.T
    rdt_ref[...] = (cum2 - jnp.log(dt) * LOG2E).T
    wt_ref[...] = w.T


def _ssd_pre(proj, dt_raw, conv_w, conv_b, dt_bias, a_log, *, nx, d, heads):
    mt = proj.shape[0]
    q, n, g = SSD_CHUNK, SSD_STATE, SSD_GROUPS
    cc = d + 2 * g * n
    rp = d // g
    nb, nbx = mt // q, nx // q
    per = q // HALO
    nhb = mt // HALO
    h2 = 2 * heads
    taps = conv_w.shape[0]
    assert taps // 2 <= HALO
    k_idx = np.asarray([k for k in range(taps) if k != taps // 2])[:, None, None]
    t_idx = np.arange(q)[None, :, None]
    r_idx = np.arange(q + 2 * HALO)[None, None, :]
    shift = jnp.asarray((r_idx == HALO - taps // 2 + k_idx + t_idx).astype(np.float32)
                        .reshape((taps - 1) * q, q + 2 * HALO)).astype(BF16)
    kern = functools.partial(_ssd_pre_kernel, nbx=nbx, nb=nb, d=d, rp=rp, heads=heads)
    return pl.pallas_call(
        kern,
        out_shape=(jax.ShapeDtypeStruct((g, mt, rp), BF16),
                   jax.ShapeDtypeStruct((g, n, mt), BF16),
                   jax.ShapeDtypeStruct((g, mt, n), BF16),
                   jax.ShapeDtypeStruct((nb, h2, q), F32),
                   jax.ShapeDtypeStruct((nb, h2, q), F32),
                   jax.ShapeDtypeStruct((nb, h2, q), F32)),
        grid=(nb,),
        in_specs=[pl.BlockSpec((q, cc), lambda i: (i, 0)),
                  pl.BlockSpec((HALO, cc), lambda i: (jnp.maximum(i * per - 1, 0), 0)),
                  pl.BlockSpec((HALO, cc), lambda i: (jnp.minimum((i + 1) * per, nhb - 1), 0)),
                  pl.BlockSpec((q, h2), lambda i: (i, 0)),
                  pl.BlockSpec(shift.shape, lambda i: (0, 0)),
                  pl.BlockSpec(conv_w.shape, lambda i: (0, 0)),
                  pl.BlockSpec((1, cc), lambda i: (0, 0)),
                  pl.BlockSpec((1, h2), lambda i: (0, 0)),
                  pl.BlockSpec((1, h2), lambda i: (0, 0))],
        out_specs=(pl.BlockSpec((g, q, rp), lambda i: (0, i, 0)),
                   pl.BlockSpec((g, n, q), lambda i: (0, 0, i)),
                   pl.BlockSpec((g, q, n), lambda i: (0, i, 0)),
                   pl.BlockSpec((None, h2, q), lambda i: (i, 0, 0)),
                   pl.BlockSpec((None, h2, q), lambda i: (i, 0, 0)),
                   pl.BlockSpec((None, h2, q), lambda i: (i, 0, 0))),
        compiler_params=_params("parallel"),
        name="ssd_pre",
    )(proj, proj, proj, dt_raw, shift, conv_w, conv_b.reshape(1, cc), dt_bias.reshape(1, h2),
      a_log.reshape(1, h2))


def _ssd_sweep_kernel(order_ref, xs_ref, bt_ref, c_ref, cumt_ref, rdt_ref, wt_ref, *rest,
                      rev, heads, hpg, d):
    del order_ref
    q, n, g_cnt = SSD_CHUNK, SSD_STATE, SSD_GROUPS
    if rev:
        yf_ref, za_ref, dexp_ref, ng_ref, out_ref, h_ref, yall_ref = rest
    else:
        out_ref, h_ref = rest
    pw = LANES
    p = pw // 2
    rp = hpg * p
    hoff = heads if rev else 0

    @pl.when(pl.program_id(0) == 0)
    def _():
        h_ref[...] = jnp.zeros_like(h_ref)

    row = lax.broadcasted_iota(jnp.int32, (q, q), 0)
    col = lax.broadcasted_iota(jnp.int32, (q, q), 1)
    tri = (col >= row) if rev else (col <= row)
    left = lax.broadcasted_iota(jnp.int32, (q, pw), 1) < p
    left1 = left[0:1, :]
    edge = 0 if rev else q - 1

    def group_body(g, carry):
        xs_g = xs_ref[g]
        c_g = c_ref[g]
        bt_g = bt_ref[g]
        s_mat = jnp.dot(c_g, bt_g, preferred_element_type=F32)
        c_f = c_g.astype(F32)
        bt_f = bt_g.astype(F32)
        for pair in range(hpg // 2):
            sl = slice(pair * pw, (pair + 1) * pw)
            xs2 = xs_g[:, sl]
            zero_x = jnp.zeros_like(xs2)
            xs_l = jnp.where(left, xs2, zero_x)
            xs_r = jnp.where(left, zero_x, xs2)
            h2 = h_ref[g, :, sl]
            h2b = h2.astype(BF16)
            zero_h = jnp.zeros_like(h2b)
            hs_l = jnp.where(left, h2b, zero_h)
            hs_r = jnp.where(left, zero_h, h2b)
            w_l, cs_l, bts_l, cd_l = [], [], [], []
            for e in range(2):
                hidx = hoff + g * hpg + 2 * pair + e
                colb = jnp.broadcast_to(cumt_ref[pl.ds(hidx, 1), :], (q, q)).T
                rowq = rdt_ref[pl.ds(hidx, 1), :]
                roww = wt_ref[pl.ds(hidx, 1), :]
                dec = jnp.exp2(jnp.where(tri, colb - rowq, -1e30))
                w_l.append((s_mat * dec).astype(BF16))
                ecol = jnp.exp2(colb)
                cs_l.append((c_f * ecol).astype(BF16))
                bts_l.append((bt_f * roww).astype(BF16))
                cd_l.append(ecol[edge:edge + 1, :])
            lhs_y = jnp.concatenate(w_l + cs_l, axis=1)
            rhs_y = jnp.concatenate([xs_l, xs_r, hs_l, hs_r], axis=0)
            y2 = jnp.dot(lhs_y, rhs_y, preferred_element_type=F32)
            st2 = jnp.dot(jnp.concatenate(bts_l, axis=1), jnp.concatenate([xs_l, xs_r], axis=0),
                          preferred_element_type=F32)
            cd2 = jnp.where(left1, cd_l[0], cd_l[1])
            h_ref[g, :, sl] = h2 * cd2 + st2
            if rev:
                yall_ref[g, :, sl] = y2 + yf_ref[g, :, sl]
            else:
                out_ref[g, :, sl] = y2
        return carry

    lax.fori_loop(0, g_cnt, group_body, 0, unroll=True)

    if rev:
        ssq = jnp.zeros((q, 1), F32)
        for g in range(g_cnt):
            gs = slice(g * rp, (g + 1) * rp)
            y = yall_ref[g] + dexp_ref[:, gs] * xs_ref[g].astype(F32)
            t = y * _silu(za_ref[:, gs].astype(F32))
            yall_ref[g] = t
            ssq = ssq + jnp.sum(t * t, axis=1, keepdims=True)
        scale = lax.rsqrt(ssq / d + EPS)
        for g in range(g_cnt):
            gs = slice(g * rp, (g + 1) * rp)
            out_ref[:, gs] = (yall_ref[g] * scale * ng_ref[:, gs]).astype(out_ref.dtype)


def _ssd_sweep(order, xs, bt, cm, cumt, rdt, wt, *, rev, heads, d, extra=None):
    g, mt, rp = xs.shape
    q, n = SSD_CHUNK, SSD_STATE
    nb = mt // q
    h2 = 2 * heads
    hpg = heads // g
    assert hpg % 2 == 0 and rp == hpg * (LANES // 2)
    in_specs = [pl.BlockSpec((g, q, rp), lambda s, o: (0, o[s], 0)),
                pl.BlockSpec((g, n, q), lambda s, o: (0, 0, o[s])),
                pl.BlockSpec((g, q, n), lambda s, o: (0, o[s], 0)),
                pl.BlockSpec((None, h2, q), lambda s, o: (o[s], 0, 0)),
                pl.BlockSpec((None, h2, q), lambda s, o: (o[s], 0, 0)),
                pl.BlockSpec((None, h2, q), lambda s, o: (o[s], 0, 0))]
    args = [xs, bt, cm, cumt, rdt, wt]
    scratch = [pltpu.VMEM((g, n, rp), F32)]
    if rev:
        yf, proj, za_blk, dexp, ng = extra
        in_specs += [pl.BlockSpec((g, q, rp), lambda s, o: (0, o[s], 0)),
                     pl.BlockSpec((q, d), lambda s, o: (o[s], za_blk)),
                     pl.BlockSpec((1, d), lambda s, o: (0, 0)),
                     pl.BlockSpec((1, d), lambda s, o: (0, 0))]
        args += [yf, proj, dexp, ng]
        out_shape = jax.ShapeDtypeStruct((mt, d), BF16)
        out_spec = pl.BlockSpec((q, d), lambda s, o: (o[s], 0))
        scratch.append(pltpu.VMEM((g, q, rp), F32))
    else:
        out_shape = jax.ShapeDtypeStruct((g, mt, rp), F32)
        out_spec = pl.BlockSpec((g, q, rp), lambda s, o: (0, o[s], 0))
    kern = functools.partial(_ssd_sweep_kernel, rev=rev, heads=heads, hpg=hpg, d=d)
    return pl.pallas_call(
        kern,
        out_shape=out_shape,
        grid_spec=pltpu.PrefetchScalarGridSpec(
            num_scalar_prefetch=1, grid=(nb,), in_specs=in_specs, out_specs=out_spec,
            scratch_shapes=scratch),
        compiler_params=_params("arbitrary"),
        name="ssd_sweep_bwd" if rev else "ssd_sweep_fwd",
    )(order, *args)


def _conformer_kernel(glu_ref, zb_ref, sh_ref, cw_ref, cb_ref, lg_ref, lb_ref, o_ref, pad_ref, v_ref,
                      *, seg, nseg, cf):
    taps = cw_ref.shape[0]
    half = taps // 2
    rc = min(seg, 64)
    ct = 2 * LANES
    rpad = seg + 2 * HALO
    u = glu_ref[:, 0:cf].astype(F32) * jax.nn.sigmoid(glu_ref[:, cf:2 * cf].astype(F32))
    for s in range(nseg):
        us = u[s * seg:(s + 1) * seg, :]
        hi = us.astype(BF16)
        lo = (us - hi.astype(F32)).astype(BF16)
        shifted = jnp.dot(sh_ref[...], jnp.concatenate([hi, lo], axis=0), preferred_element_type=F32)
        pad_ref[s] = shifted.reshape(SUBLANES, rpad, cf)
    for c0 in range(0, cf, ct):
        for s in range(nseg):
            for r0 in range(0, seg, rc):
                acc = jnp.broadcast_to(cb_ref[:, c0:c0 + ct], (rc, ct))
                for k in range(taps):
                    off = HALO + k - half
                    ph = off % SUBLANES
                    base = off - ph + r0
                    acc = acc + pad_ref[s, ph, base:base + rc, c0:c0 + ct] * cw_ref[k:k + 1, c0:c0 + ct]
                v_ref[s * seg + r0:s * seg + r0 + rc, c0:c0 + ct] = acc
    v = v_ref[...]
    mu = jnp.mean(v, axis=-1, keepdims=True)
    xc = v - mu
    y = xc * lax.rsqrt(jnp.mean(xc * xc, axis=-1, keepdims=True) + EPS) * lg_ref[...] + lb_ref[...]
    o_ref[...] = (_silu(y) * _silu(zb_ref[...].astype(F32))).astype(o_ref.dtype)


def _conformer(proj, conv_w, conv_b, ln_g, ln_b, *, row_off, rows, seg, glu_off, zb_off):
    taps, cf = conv_w.shape
    assert taps // 2 <= HALO
    tb = seg if seg >= 128 else 128
    nseg = tb // seg
    roff = row_off // tb
    rpad = seg + 2 * HALO
    r_idx = np.arange(rpad)[None, :, None]
    ph_idx = np.arange(SUBLANES)[:, None, None]
    t_idx = np.arange(seg)[None, None, :]
    shift = (r_idx == HALO - ph_idx + t_idx).astype(np.float32).reshape(SUBLANES * rpad, seg)
    shift2 = jnp.asarray(np.concatenate([shift, shift], axis=1)).astype(BF16)
    kern = functools.partial(_conformer_kernel, seg=seg, nseg=nseg, cf=cf)
    return pl.pallas_call(
        kern,
        out_shape=jax.ShapeDtypeStruct((rows, cf), BF16),
        grid=(rows // tb,),
        in_specs=[pl.BlockSpec((tb, 2 * cf), lambda i: (i + roff, glu_off // (2 * cf))),
                  pl.BlockSpec((tb, cf), lambda i: (i + roff, zb_off // cf)),
                  pl.BlockSpec(shift2.shape, lambda i: (0, 0)),
                  pl.BlockSpec((taps, cf), lambda i: (0, 0)),
                  pl.BlockSpec((1, cf), lambda i: (0, 0)),
                  pl.BlockSpec((1, cf), lambda i: (0, 0)),
                  pl.BlockSpec((1, cf), lambda i: (0, 0))],
        out_specs=pl.BlockSpec((tb, cf), lambda i: (i, 0)),
        scratch_shapes=[pltpu.VMEM((nseg, SUBLANES, rpad, cf), F32),
                        pltpu.VMEM((tb, cf), F32)],
        compiler_params=_params("parallel"),
        name="conformer",
    )(proj, proj, shift2, conv_w, conv_b.reshape(1, cf), ln_g.reshape(1, cf), ln_b.reshape(1, cf))


def _dft_cs(n):
    idx = np.arange(n)
    ang = 2.0 * np.pi * ((idx[:, None] * idx[None, :]) % n) / n
    return np.cos(ang), np.sin(ang)


def _fft_tables(l, gd):
    l1 = FFT_L1
    l2 = l // l1
    s = SUBLANES
    eye = np.eye(s)
    c1, s1 = _dft_cs(l1)
    c2, s2 = _dft_cs(l2)

    def kron_a(m):
        m4 = m.reshape(l1 // s, 1, s, l1, 1)
        return (m4 * eye.reshape(1, s, 1, 1, s)).reshape(l1 * s, l1 * s)

    def const_bf16(m):
        return jnp.asarray(m.astype(np.float32)).astype(BF16)

    ka = const_bf16(np.concatenate([kron_a(c1), kron_a(s1)], axis=0))
    mb = const_bf16(np.block([[c2, -s2], [s2, c2]]))
    k1 = np.arange(l1)[:, None]
    t2 = np.arange(l2)[None, :]
    ang = 2.0 * np.pi * ((k1 * t2) % l) / l

    def tw(m):
        m4 = m.astype(np.float32).reshape(l1 // s, s, l2, 1)
        return jnp.asarray(np.ascontiguousarray(np.broadcast_to(m4, (l1 // s, s, l2, LANES))))

    return ka, mb, tw(np.cos(ang)), tw(np.sin(ang)), _chan_table(l, gd)


def _chan_table(l, gd):
    cc, sc = _dft_cs(gd)
    return jnp.asarray(np.concatenate([cc, -sc], axis=0) / math.sqrt(l * gd), F32).astype(BF16)


def _fft_a_kernel(v_ref, ka_ref, or_ref, oi_ref):
    l1, rows, cb = v_ref.shape
    s = SUBLANES
    nkb = or_ref.shape[0]
    xf = v_ref[...].astype(F32)
    for h in range(rows // s):
        xb = xf[:, h * s:(h + 1) * s, :].reshape(l1 * s, cb).astype(BF16)
        y = jnp.dot(ka_ref[...], xb, preferred_element_type=F32)
        or_ref[:, h * s:(h + 1) * s, :, :] = y[0:l1 * s].reshape(nkb, s, s, cb)
        oi_ref[:, h * s:(h + 1) * s, :, :] = y[l1 * s:2 * l1 * s].reshape(nkb, s, s, cb)


def _fft_b_kernel(*refs, nh):
    ar_refs, ai_refs = refs[0:nh], refs[nh:2 * nh]
    twc_ref, tws_ref, mb_ref, tc_ref, o_ref = refs[2 * nh:2 * nh + 5]
    f_refs = refs[2 * nh + 5:]
    l2, s, _ = o_ref.shape
    for r in range(s):
        c, sn = twc_ref[r], tws_ref[r]
        br, bi = [], []
        for h in range(nh):
            ar = ar_refs[h][pl.ds(r, l2, stride=s), :]
            ai = ai_refs[h][pl.ds(r, l2, stride=s), :]
            br.append(ar * c - ai * sn)
            bi.append(ar * sn + ai * c)
        stack = jnp.concatenate([jnp.concatenate(br, axis=1), jnp.concatenate(bi, axis=1)], axis=0)
        z = jnp.dot(mb_ref[...], stack.astype(BF16), preferred_element_type=F32)
        xri = jnp.concatenate([z[0:l2], z[l2:2 * l2]], axis=1).astype(BF16)
        f = jnp.dot(xri, tc_ref[...], preferred_element_type=F32)
        for h in range(nh):
            f_refs[h][pl.ds(r, l2, stride=s), :] = f[:, h * LANES:(h + 1) * LANES]
    for h in range(nh):
        o_ref[:, :, h * LANES:(h + 1) * LANES] = f_refs[h][...].reshape(l2, s, LANES)


def _fourier_x(proj, tables, *, l, c, v_off):
    ka, mb, twc, tws, tc = tables
    l1, s = FFT_L1, SUBLANES
    l2 = l // l1
    gd = c // FN_GROUPS
    nkb = l1 // s
    mt, ncols = proj.shape
    tr = BF16_SUBLANES
    assert mt % l2 == 0 and l2 % tr == 0
    ap_shape = jax.ShapeDtypeStruct((nkb, l2, s, c), F32)
    voff = v_off // gd
    apr, api = pl.pallas_call(
        _fft_a_kernel,
        out_shape=(ap_shape, ap_shape),
        grid=(l2 // tr, c // gd),
        in_specs=[pl.BlockSpec((l1, tr, gd), lambda i, j: (0, i, voff + j)),
                  pl.BlockSpec(ka.shape, lambda i, j: (0, 0))],
        out_specs=(pl.BlockSpec((nkb, tr, s, gd), lambda i, j: (0, i, 0, j)),
                   pl.BlockSpec((nkb, tr, s, gd), lambda i, j: (0, i, 0, j))),
        compiler_params=_params("parallel", "parallel"),
        name="fft_stage_a",
    )(proj.reshape(mt // l2, l2, ncols), ka)
    nh = gd // LANES
    slab_specs = [pl.BlockSpec((None, l2 * s, LANES), lambda kq, j, h=h: (kq, 0, j * nh + h))
                  for h in range(nh)]
    apr2, api2 = apr.reshape(nkb, l2 * s, c), api.reshape(nkb, l2 * s, c)
    f = pl.pallas_call(
        functools.partial(_fft_b_kernel, nh=nh),
        out_shape=jax.ShapeDtypeStruct((l2, nkb, s, c), F32),
        grid=(nkb, c // gd),
        in_specs=slab_specs + slab_specs + [
            pl.BlockSpec((None, s, l2, LANES), lambda kq, j: (kq, 0, 0, 0)),
            pl.BlockSpec((None, s, l2, LANES), lambda kq, j: (kq, 0, 0, 0)),
            pl.BlockSpec(mb.shape, lambda kq, j: (0, 0)),
            pl.BlockSpec(tc.shape, lambda kq, j: (0, 0))],
        out_specs=pl.BlockSpec((l2, None, s, gd), lambda kq, j: (0, kq, 0, j)),
        scratch_shapes=[pltpu.VMEM((l2 * s, LANES), F32)] * nh,
        compiler_params=_params("parallel", "parallel"),
        name="fft_stage_b",
    )(*([apr2] * nh + [api2] * nh), twc, tws, mb, tc)
    return f.reshape(l, c)


def _gate_mul_kernel(f_ref, z_ref, o_ref):
    o_ref[...] = (f_ref[...] * _silu(z_ref[...].astype(F32))).astype(o_ref.dtype)


def _gate_mul(f, proj, zc_off):
    l, c = f.shape
    tm = _pick_tile(l, 512, BF16_SUBLANES)
    return pl.pallas_call(
        _gate_mul_kernel,
        out_shape=jax.ShapeDtypeStruct((l, c), BF16),
        grid=(l // tm,),
        in_specs=[pl.BlockSpec((tm, c), lambda i: (i, 0)),
                  pl.BlockSpec((tm, c), lambda i: (i, zc_off // c))],
        out_specs=pl.BlockSpec((tm, c), lambda i: (i, 0)),
        compiler_params=_params("parallel"),
        name="fourier_gate",
    )(f, proj)


def _fft_ctx_kernel(v_ref, z_ref, cs_ref, tc_ref, o_ref):
    lc = v_ref.shape[0]
    p = jnp.dot(cs_ref[...], v_ref[...], preferred_element_type=F32)
    pri = jnp.concatenate([p[0:lc], p[lc:2 * lc]], axis=1).astype(BF16)
    f = jnp.dot(pri, tc_ref[...], preferred_element_type=F32)
    o_ref[...] = (f * _silu(z_ref[...].astype(F32))).astype(o_ref.dtype)


def _fourier_ctx(proj, *, row_off, lc, c, v_off, zc_off):
    gd = c // FN_GROUPS
    cc, sc = _dft_cs(lc)
    cs = jnp.asarray(np.concatenate([cc, sc], axis=0), F32).astype(BF16)
    tc = _chan_table(lc, gd)
    rblk = row_off // lc
    return pl.pallas_call(
        _fft_ctx_kernel,
        out_shape=jax.ShapeDtypeStruct((lc, c), BF16),
        grid=(FN_GROUPS,),
        in_specs=[pl.BlockSpec((lc, gd), lambda g: (rblk, v_off // gd + g)),
                  pl.BlockSpec((lc, gd), lambda g: (rblk, zc_off // gd + g)),
                  pl.BlockSpec(cs.shape, lambda g: (0, 0)),
                  pl.BlockSpec(tc.shape, lambda g: (0, 0))],
        out_specs=pl.BlockSpec((lc, gd), lambda g: (0, g)),
        compiler_params=_params("parallel"),
        name="fourier_ctx",
    )(proj, proj, cs, tc)


def _merge_kernel(ya_ref, yb_ref, yc_ref, ga_ref, gb_ref, gc_ref, wa_ref, wb_ref, wc_ref, o_ref):
    def term(y_ref, w_ref, g_ref):
        return jax.nn.sigmoid(g_ref[...].astype(F32)) * jnp.dot(
            y_ref[...], w_ref[...], preferred_element_type=F32)
    acc = term(ya_ref, wa_ref, ga_ref) + term(yb_ref, wb_ref, gb_ref) + term(yc_ref, wc_ref, gc_ref)
    o_ref[...] = acc.astype(o_ref.dtype)


def _merge(ya, yb, yc, proj, w_branch, layer, *, rows, ya_row_off, proj_row_off, gate_off):
    d = ya.shape[1]
    cf, fw = yb.shape[1], yc.shape[1]
    tm = _pick_tile(rows, 1024, LANES)
    tn = _pick_tile(d, 256, LANES)
    ra, rpj = ya_row_off // tm, proj_row_off // tm
    gblk = gate_off // tn
    dblk = d // tn
    return pl.pallas_call(
        _merge_kernel,
        out_shape=jax.ShapeDtypeStruct((rows, d), BF16),
        grid=(rows // tm, d // tn),
        in_specs=[pl.BlockSpec((tm, d), lambda i, j: (i + ra, 0)),
                  pl.BlockSpec((tm, cf), lambda i, j: (i, 0)),
                  pl.BlockSpec((tm, fw), lambda i, j: (i, 0)),
                  pl.BlockSpec((tm, tn), lambda i, j: (i + rpj, gblk + j)),
                  pl.BlockSpec((tm, tn), lambda i, j: (i + rpj, gblk + dblk + j)),
                  pl.BlockSpec((tm, tn), lambda i, j: (i + rpj, gblk + 2 * dblk + j)),
                  pl.BlockSpec((None, d, tn), lambda i, j: (layer, 0, j)),
                  pl.BlockSpec((None, cf, tn), lambda i, j: (layer, d // cf, j)),
                  pl.BlockSpec((None, fw, tn), lambda i, j: (layer, (d + cf) // fw, j))],
        out_specs=pl.BlockSpec((tm, tn), lambda i, j: (i, j)),
        compiler_params=_params("parallel", "parallel"),
        name="merge",
    )(ya, yb, yc, proj, proj, proj, w_branch, w_branch, w_branch)


def _column_layout(d, cf, fw, heads):
    gn = SSD_GROUPS * SSD_STATE
    cc = d + 2 * gn
    src = {}
    pos = 0
    for name, width in (("xbc", cc), ("za", d), ("dt", 2 * heads), ("glu", 2 * cf), ("zb", cf),
                        ("v", fw), ("zc", fw), ("gates", 3 * d)):
        src[name] = (pos, width)
        pos += width
    order = (("xbc", cc), ("zb", cf), ("v", fw), ("zc", fw), ("za", d), ("glu", 2 * cf),
             ("gates", d), ("dt", 2 * heads))
    dst = {}
    pieces = []
    pos = 0
    for name, align in order:
        padw = (-pos) % align
        if padw:
            pieces.append(("pad", padw))
            pos += padw
        dst[name] = pos
        pieces.append((name, src[name]))
        pos += src[name][1]
    return dst, pieces, pos


def _source_columns(pieces):
    main = [(name, spec) for name, spec in pieces if name != "dt"]
    widths = [spec if name == "pad" else spec[1] for name, spec in main]
    tn = _pick_tile(functools.reduce(math.gcd, widths), 512, LANES)
    cols = []
    for name, spec in main:
        start, width = (0, spec) if name == "pad" else spec
        cols += [start + k * tn if name != "pad" else 0 for k in range(width // tn)]
    return tn, cols


def kernel(x, c, ctx, c_ctx, w_mod, b_mod, norm_g, w_in, ssd_conv_w, ssd_conv_b, ssd_dt_bias,
           ssd_a_log, ssd_d, ssd_norm_g, cf_conv_w, cf_conv_b, cf_ln_g, cf_ln_b, w_branch,
           w_out, final_g):
    bsz, l, d = x.shape
    assert bsz == 1, "single-sample kernel"
    lc = ctx.shape[1]
    depth = w_mod.shape[0]
    heads = ssd_dt_bias.shape[-1]
    cf = cf_conv_w.shape[-1]
    fw = w_branch.shape[1] - d - cf
    p = d // heads
    assert 2 * p == LANES and l % SSD_CHUNK == 0 and lc % SSD_CHUNK == 0 and l % lc == 0
    mt = l + lc
    nbx, nb = l // SSD_CHUNK, mt // SSD_CHUNK

    dst, pieces, _ = _column_layout(d, cf, fw, heads)
    tn_in, src_cols = _source_columns(pieces)
    dt_src = [spec[0] for name, spec in pieces if name == "dt"]

    cond8 = jnp.zeros((SUBLANES, d), F32).at[0].set(c[0]).at[1].set(c_ctx)
    mod_all = _modulation(cond8, w_mod, b_mod)

    order_f = jnp.asarray(list(range(nbx, nb)) + list(range(nbx)), jnp.int32)
    order_b = jnp.asarray(list(range(nb - 1, nbx - 1, -1)) + list(range(nbx - 1, -1, -1)), jnp.int32)
    tables = _fft_tables(l, fw // FN_GROUPS)

    xs2 = x[0]
    cs2 = ctx[0]
    w_br = w_branch.astype(BF16)
    for i in range(depth):
        update_ctx = i < depth - 1
        hcat = _norm_mod(xs2, cs2, norm_g[i], mod_all, i)
        proj = _in_proj(hcat, w_in, i, src_cols, tn_in, BF16, "in_proj")
        dt_raw = _in_proj(hcat, w_in, i, dt_src, 2 * heads, F32, "in_proj_dt")

        xs_g, bt_g, c_g, cumt, rdt, wt = _ssd_pre(
            proj, dt_raw, ssd_conv_w[i], ssd_conv_b[i], ssd_dt_bias[i].reshape(-1),
            ssd_a_log[i].reshape(-1), nx=l, d=d, heads=heads)
        y_f = _ssd_sweep(order_f, xs_g, bt_g, c_g, cumt, rdt, wt, rev=False, heads=heads, d=d)
        dexp = jnp.repeat(ssd_d[i], p).reshape(1, d)
        ya = _ssd_sweep(order_b, xs_g, bt_g, c_g, cumt, rdt, wt, rev=True, heads=heads, d=d,
                        extra=(y_f, proj, dst["za"] // d, dexp, ssd_norm_g[i].reshape(1, d)))

        yb = _conformer(proj, cf_conv_w[i], cf_conv_b[i], cf_ln_g[i], cf_ln_b[i],
                        row_off=0, rows=l, seg=GRID_W, glu_off=dst["glu"], zb_off=dst["zb"])
        yc = _gate_mul(_fourier_x(proj, tables, l=l, c=fw, v_off=dst["v"]), proj, dst["zc"])
        merged = _merge(ya, yb, yc, proj, w_br, i, rows=l, ya_row_off=0, proj_row_off=0,
                        gate_off=dst["gates"])
        xs2 = _out_proj(merged, w_out, xs2, mod_all, i, 0)
        if update_ctx:
            yb_c = _conformer(proj, cf_conv_w[i], cf_conv_b[i], cf_ln_g[i], cf_ln_b[i],
                              row_off=l, rows=lc, seg=lc, glu_off=dst["glu"], zb_off=dst["zb"])
            yc_c = _fourier_ctx(proj, row_off=l, lc=lc, c=fw, v_off=dst["v"], zc_off=dst["zc"])
            merged_c = _merge(ya, yb_c, yc_c, proj, w_br, i, rows=lc, ya_row_off=l, proj_row_off=l,
                              gate_off=dst["gates"])
            cs2 = _out_proj(merged_c, w_out, cs2, mod_all, i, 1)
    return _final_norm(xs2, final_g)[None]
```

```python
import functools
import math

import numpy as np
import jax
import jax.numpy as jnp
from jax import lax
from jax.experimental import pallas as pl
from jax.experimental.pallas import tpu as pltpu

F32 = jnp.float32
BF16 = jnp.bfloat16

SSD_GROUPS = 8
SSD_STATE = 128
SSD_CHUNK = 128
GRID_W = 64
FN_GROUPS = 8
EPS = 1e-6
FFT_L1 = 64
LOG2E = math.log2(math.e)

SUBLANES = 8
LANES = 128
BF16_SUBLANES = 16
VMEM_LIMIT_BYTES = 56 * 1024 * 1024
HALO = 16


def _params(*sem):
    return pltpu.CompilerParams(dimension_semantics=sem, vmem_limit_bytes=VMEM_LIMIT_BYTES)


def _pick_tile(n, cap, mult):
    best = None
    t = mult
    while t <= min(n, cap):
        if n % t == 0:
            best = t
        t += mult
    assert best is not None, (n, cap, mult)
    return best


def _silu(v):
    return v * jax.nn.sigmoid(v)


def _mod_kernel(s_ref, w_ref, b_ref, o_ref):
    s = _silu(s_ref[...])
    o_ref[...] = jnp.dot(s.astype(BF16), w_ref[...].astype(BF16),
                         preferred_element_type=F32) + b_ref[...]


def _modulation(cond8, w_mod, b_mod):
    depth, d, n3 = w_mod.shape
    tn = _pick_tile(n3, 512, LANES)
    return pl.pallas_call(
        _mod_kernel,
        out_shape=jax.ShapeDtypeStruct((depth, SUBLANES, n3), F32),
        grid=(depth, n3 // tn),
        in_specs=[pl.BlockSpec((SUBLANES, d), lambda l, j: (0, 0)),
                  pl.BlockSpec((None, d, tn), lambda l, j: (l, 0, j)),
                  pl.BlockSpec((None, 1, tn), lambda l, j: (l, 0, j))],
        out_specs=pl.BlockSpec((None, SUBLANES, tn), lambda l, j: (l, 0, j)),
        compiler_params=_params("parallel", "parallel"),
        name="modulation",
    )(cond8, w_mod, b_mod.reshape(depth, 1, n3))


def _norm_mod_kernel(x_ref, c_ref, g_ref, sh_ref, sc_ref, o_ref, *, nbx):
    def emit(v_ref, row):
        v = v_ref[...]
        y = v * lax.rsqrt(jnp.mean(v * v, axis=-1, keepdims=True) + EPS) * g_ref[...]
        o_ref[...] = (y * (1.0 + sc_ref[row:row + 1, :]) + sh_ref[row:row + 1, :]).astype(o_ref.dtype)

    is_x = pl.program_id(0) < nbx
    pl.when(is_x)(lambda: emit(x_ref, 0))
    pl.when(jnp.logical_not(is_x))(lambda: emit(c_ref, 1))


def _norm_mod(x2, c2, g, mod_all, layer):
    l, d = x2.shape
    lc = c2.shape[0]
    tm = _pick_tile(math.gcd(l, lc), 256, BF16_SUBLANES)
    nbx, nbc = l // tm, lc // tm
    return pl.pallas_call(
        functools.partial(_norm_mod_kernel, nbx=nbx),
        out_shape=jax.ShapeDtypeStruct((l + lc, d), BF16),
        grid=(nbx + nbc,),
        in_specs=[pl.BlockSpec((tm, d), lambda i: (jnp.minimum(i, nbx - 1), 0)),
                  pl.BlockSpec((tm, d), lambda i: (jnp.maximum(i - nbx, 0), 0)),
                  pl.BlockSpec((1, d), lambda i: (0, 0)),
                  pl.BlockSpec((None, SUBLANES, d), lambda i: (layer, 0, 0)),
                  pl.BlockSpec((None, SUBLANES, d), lambda i: (layer, 0, 1))],
        out_specs=pl.BlockSpec((tm, d), lambda i: (i, 0)),
        compiler_params=_params("parallel"),
        name="norm_mod",
    )(x2, c2, g.reshape(1, d), mod_all, mod_all)


def _final_norm_kernel(x_ref, g_ref, o_ref):
    x = x_ref[...]
    o_ref[...] = x * lax.rsqrt(jnp.mean(x * x, axis=-1, keepdims=True) + EPS) * g_ref[...]


def _final_norm(x2, g):
    m, d = x2.shape
    tm = _pick_tile(m, 256, SUBLANES)
    return pl.pallas_call(
        _final_norm_kernel,
        out_shape=jax.ShapeDtypeStruct((m, d), F32),
        grid=(m // tm,),
        in_specs=[pl.BlockSpec((tm, d), lambda i: (i, 0)),
                  pl.BlockSpec((1, d), lambda i: (0, 0))],
        out_specs=pl.BlockSpec((tm, d), lambda i: (i, 0)),
        compiler_params=_params("parallel"),
        name="final_norm",
    )(x2, g.reshape(1, d))


def _in_proj_kernel(src_ref, a_ref, w_ref, o_ref, wb_ref):
    del src_ref

    @pl.when(pl.program_id(1) == 0)
    def _():
        wb_ref[...] = w_ref[...].astype(BF16)

    o_ref[...] = jnp.dot(a_ref[...], wb_ref[...], preferred_element_type=F32).astype(o_ref.dtype)


def _in_proj(a, w_in, layer, src_cols, tn, out_dtype, name):
    m, k = a.shape
    nblk = len(src_cols)
    tm = _pick_tile(m, 1408, LANES)
    unit = LANES if all(c % LANES == 0 for c in src_cols) else 1
    src = jnp.asarray([c // unit for c in src_cols], jnp.int32)
    return pl.pallas_call(
        _in_proj_kernel,
        out_shape=jax.ShapeDtypeStruct((m, nblk * tn), out_dtype),
        grid_spec=pltpu.PrefetchScalarGridSpec(
            num_scalar_prefetch=1, grid=(nblk, m // tm),
            in_specs=[pl.BlockSpec((tm, k), lambda j, i, s: (i, 0)),
                      pl.BlockSpec((None, pl.Element(k), pl.Element(tn)),
                                   lambda j, i, s: (layer, 0, s[j] * unit))],
            out_specs=pl.BlockSpec((tm, tn), lambda j, i, s: (i, j)),
            scratch_shapes=[pltpu.VMEM((k, tn), BF16)]),
        compiler_params=_params("parallel", "arbitrary"),
        name=name,
    )(src, a, w_in)


def _out_proj_kernel(a_ref, w_ref, r_ref, g_ref, o_ref, wb_ref, *, row):
    @pl.when(pl.program_id(1) == 0)
    def _():
        wb_ref[...] = w_ref[...].astype(BF16)

    acc = jnp.dot(a_ref[...], wb_ref[...], preferred_element_type=F32)
    o_ref[...] = r_ref[...] + g_ref[row:row + 1, :] * acc


def _out_proj(merged, w_out, resid, mod_all, layer, row):
    m, d = resid.shape
    tm = _pick_tile(m, 1024, LANES)
    tn = _pick_tile(d, 512, LANES)
    goff = 2 * d // tn
    return pl.pallas_call(
        functools.partial(_out_proj_kernel, row=row),
        out_shape=jax.ShapeDtypeStruct((m, d), F32),
        grid=(d // tn, m // tm),
        in_specs=[pl.BlockSpec((tm, d), lambda j, i: (i, 0)),
                  pl.BlockSpec((None, d, tn), lambda j, i: (layer, 0, j)),
                  pl.BlockSpec((tm, tn), lambda j, i: (i, j)),
                  pl.BlockSpec((None, SUBLANES, tn), lambda j, i: (layer, 0, j + goff))],
        out_specs=pl.BlockSpec((tm, tn), lambda j, i: (i, j)),
        scratch_shapes=[pltpu.VMEM((d, tn), BF16)],
        compiler_params=_params("parallel", "arbitrary"),
        name="out_proj",
    )(merged, w_out, resid, mod_all)


def _split3(a):
    hi = a.astype(BF16)
    r1 = a - hi.astype(F32)
    mid = r1.astype(BF16)
    lo = (r1 - mid.astype(F32)).astype(BF16)
    return hi, mid, lo


def _ssd_pre_kernel(cur_ref, prev_ref, next_ref, dtraw_ref, sh_ref, cw_ref, cb_ref, dtb_ref, alog_ref,
                    xs_ref, bt_ref, c_ref, cumt_ref, rdt_ref, wt_ref,
                    *, nbx, nb, d, rp, heads):
    q = SSD_CHUNK
    n = SSD_STATE
    i = pl.program_id(0)
    first = jnp.logical_or(i == 0, i == nbx)
    last = jnp.logical_or(i == nbx - 1, i == nb - 1)
    pmask = jnp.where(first, 0.0, 1.0).astype(BF16)
    nmask = jnp.where(last, 0.0, 1.0).astype(BF16)
    taps = cw_ref.shape[0]

    def conv_silu(c0, width):
        cols = slice(c0, c0 + width)
        ext = jnp.concatenate([prev_ref[:, cols] * pmask, cur_ref[:, cols], next_ref[:, cols] * nmask],
                              axis=0)
        win = jnp.dot(sh_ref[...], ext, preferred_element_type=F32)
        acc = cb_ref[:, cols] + cur_ref[:, cols].astype(F32) * cw_ref[taps // 2:taps // 2 + 1, cols]
        for j, k in enumerate(kk for kk in range(taps) if kk != taps // 2):
            acc = acc + win[j * q:(j + 1) * q, :] * cw_ref[k:k + 1, cols]
        return _silu(acc)

    for g in range(SSD_GROUPS):
        xs_ref[g] = conv_silu(g * rp, rp).astype(xs_ref.dtype)
    for g in range(SSD_GROUPS):
        bt_ref[g] = conv_silu(d + g * n, n).T.astype(bt_ref.dtype)
    for g in range(SSD_GROUPS):
        c_ref[g] = conv_silu(d + SSD_GROUPS * n + g * n, n).astype(c_ref.dtype)

    raw = dtraw_ref[...] + dtb_ref[...]
    dt = jnp.maximum(raw, 0.0) + jnp.log1p(jnp.exp(-jnp.abs(raw)))
    a = dt * (-jnp.exp(alog_ref[...]))
    row = lax.broadcasted_iota(jnp.int32, (q, q), 0)
    col = lax.broadcasted_iota(jnp.int32, (q, q), 1)
    tri_f = jnp.where(col <= row, 1.0, 0.0).astype(BF16)
    tri_b = jnp.where(col >= row, 1.0, 0.0).astype(BF16)
    cf = jnp.zeros((q, 2 * heads), F32)
    cb = jnp.zeros((q, 2 * heads), F32)
    for piece in _split3(a):
        cf = cf + jnp.dot(tri_f, piece, preferred_element_type=F32)
        cb = cb + jnp.dot(tri_b, piece, preferred_element_type=F32)
    is_fwd = lax.broadcasted_iota(jnp.int32, (q, 2 * heads), 1) < heads
    cum2 = jnp.where(is_fwd, cf, cb) * LOG2E
    clast = jnp.where(is_fwd[0:1, :], cum2[q - 1:q, :], cum2[0:1, :])
    w = dt * jnp.exp2(clast - cum2)
    cumt_ref[...] = cum2.T
    rdt_ref[...] = (cum2 - jnp.log(dt) * LOG2E).T
    wt_ref[...] = w.T


def _ssd_pre(proj, dt_raw, conv_w, conv_b, dt_bias, a_log, *, nx, d, heads):
    mt = proj.shape[0]
    q, n, g = SSD_CHUNK, SSD_STATE, SSD_GROUPS
    cc = d + 2 * g * n
    rp = d // g
    nb, nbx = mt // q, nx // q
    per = q // HALO
    nhb = mt // HALO
    h2 = 2 * heads
    taps = conv_w.shape[0]
    assert taps // 2 <= HALO
    k_idx = np.asarray([k for k in range(taps) if k != taps // 2])[:, None, None]
    t_idx = np.arange(q)[None, :, None]
    r_idx = np.arange(q + 2 * HALO)[None, None, :]
    shift = jnp.asarray((r_idx == HALO - taps // 2 + k_idx + t_idx).astype(np.float32)
                        .reshape((taps - 1) * q, q + 2 * HALO)).astype(BF16)
    kern = functools.partial(_ssd_pre_kernel, nbx=nbx, nb=nb, d=d, rp=rp, heads=heads)
    return pl.pallas_call(
        kern,
        out_shape=(jax.ShapeDtypeStruct((g, mt, rp), BF16),
                   jax.ShapeDtypeStruct((g, n, mt), BF16),
                   jax.ShapeDtypeStruct((g, mt, n), BF16),
                   jax.ShapeDtypeStruct((nb, h2, q), F32),
                   jax.ShapeDtypeStruct((nb, h2, q), F32),
                   jax.ShapeDtypeStruct((nb, h2, q), F32)),
        grid=(nb,),
        in_specs=[pl.BlockSpec((q, cc), lambda i: (i, 0)),
                  pl.BlockSpec((HALO, cc), lambda i: (jnp.maximum(i * per - 1, 0), 0)),
                  pl.BlockSpec((HALO, cc), lambda i: (jnp.minimum((i + 1) * per, nhb - 1), 0)),
                  pl.BlockSpec((q, h2), lambda i: (i, 0)),
                  pl.BlockSpec(shift.shape, lambda i: (0, 0)),
                  pl.BlockSpec(conv_w.shape, lambda i: (0, 0)),
                  pl.BlockSpec((1, cc), lambda i: (0, 0)),
                  pl.BlockSpec((1, h2), lambda i: (0, 0)),
                  pl.BlockSpec((1, h2), lambda i: (0, 0))],
        out_specs=(pl.BlockSpec((g, q, rp), lambda i: (0, i, 0)),
                   pl.BlockSpec((g, n, q), lambda i: (0, 0, i)),
                   pl.BlockSpec((g, q, n), lambda i: (0, i, 0)),
                   pl.BlockSpec((None, h2, q), lambda i: (i, 0, 0)),
                   pl.BlockSpec((None, h2, q), lambda i: (i, 0, 0)),
                   pl.BlockSpec((None, h2, q), lambda i: (i, 0, 0))),
        compiler_params=_params("parallel"),
        name="ssd_pre",
    )(proj, proj, proj, dt_raw, shift, conv_w, conv_b.reshape(1, cc), dt_bias.reshape(1, h2),
      a_log.reshape(1, h2))


def _ssd_sweep_kernel(order_ref, xs_ref, bt_ref, c_ref, cumt_ref, rdt_ref, wt_ref, *rest,
                      rev, heads, hpg, d):
    del order_ref
    q, n, g_cnt = SSD_CHUNK, SSD_STATE, SSD_GROUPS
    if rev:
        yf_ref, za_ref, dexp_ref, ng_ref, out_ref, h_ref, yall_ref = rest
    else:
        out_ref, h_ref = rest
    pw = LANES
    p = pw // 2
    rp = hpg * p
    hoff = heads if rev else 0

    @pl.when(pl.program_id(0) == 0)
    def _():
        h_ref[...] = jnp.zeros_like(h_ref)

    row = lax.broadcasted_iota(jnp.int32, (q, q), 0)
    col = lax.broadcasted_iota(jnp.int32, (q, q), 1)
    tri = (col >= row) if rev else (col <= row)
    left = lax.broadcasted_iota(jnp.int32, (q, pw), 1) < p
    left1 = left[0:1, :]
    edge = 0 if rev else q - 1

    def group_body(g, carry):
        xs_g = xs_ref[g]
        c_g = c_ref[g]
        bt_g = bt_ref[g]
        s_mat = jnp.dot(c_g, bt_g, preferred_element_type=F32)
        c_f = c_g.astype(F32)
        bt_f = bt_g.astype(F32)
        for pair in range(hpg // 2):
            sl = slice(pair * pw, (pair + 1) * pw)
            xs2 = xs_g[:, sl]
            zero_x = jnp.zeros_like(xs2)
            xs_l = jnp.where(left, xs2, zero_x)
            xs_r = jnp.where(left, zero_x, xs2)
            h2 = h_ref[g, :, sl]
            h2b = h2.astype(BF16)
            zero_h = jnp.zeros_like(h2b)
            hs_l = jnp.where(left, h2b, zero_h)
            hs_r = jnp.where(left, zero_h, h2b)
            w_l, cs_l, bts_l, cd_l = [], [], [], []
            for e in range(2):
                hidx = hoff + g * hpg + 2 * pair + e
                colb = jnp.broadcast_to(cumt_ref[pl.ds(hidx, 1), :], (q, q)).T
                rowq = rdt_ref[pl.ds(hidx, 1), :]
                roww = wt_ref[pl.ds(hidx, 1), :]
                dec = jnp.exp2(jnp.where(tri, colb - rowq, -1e30))
                w_l.append((s_mat * dec).astype(BF16))
                ecol = jnp.exp2(colb)
                cs_l.append((c_f * ecol).astype(BF16))
                bts_l.append((bt_f * roww).astype(BF16))
                cd_l.append(ecol[edge:edge + 1, :])
            lhs_y = jnp.concatenate(w_l + cs_l, axis=1)
            rhs_y = jnp.concatenate([xs_l, xs_r, hs_l, hs_r], axis=0)
            y2 = jnp.dot(lhs_y, rhs_y, preferred_element_type=F32)
            st2 = jnp.dot(jnp.concatenate(bts_l, axis=1), jnp.concatenate([xs_l, xs_r], axis=0),
                          preferred_element_type=F32)
            cd2 = jnp.where(left1, cd_l[0], cd_l[1])
            h_ref[g, :, sl] = h2 * cd2 + st2
            if rev:
                yall_ref[g, :, sl] = y2 + yf_ref[g, :, sl]
            else:
                out_ref[g, :, sl] = y2
        return carry

    lax.fori_loop(0, g_cnt, group_body, 0, unroll=True)

    if rev:
        ssq = jnp.zeros((q, 1), F32)
        for g in range(g_cnt):
            gs = slice(g * rp, (g + 1) * rp)
            y = yall_ref[g] + dexp_ref[:, gs] * xs_ref[g].astype(F32)
            t = y * _silu(za_ref[:, gs].astype(F32))
            yall_ref[g] = t
            ssq = ssq + jnp.sum(t * t, axis=1, keepdims=True)
        scale = lax.rsqrt(ssq / d + EPS)
        for g in range(g_cnt):
            gs = slice(g * rp, (g + 1) * rp)
            out_ref[:, gs] = (yall_ref[g] * scale * ng_ref[:, gs]).astype(out_ref.dtype)


def _ssd_sweep(order, xs, bt, cm, cumt, rdt, wt, *, rev, heads, d, extra=None):
    g, mt, rp = xs.shape
    q, n = SSD_CHUNK, SSD_STATE
    nb = mt // q
    h2 = 2 * heads
    hpg = heads // g
    assert hpg % 2 == 0 and rp == hpg * (LANES // 2)
    in_specs = [pl.BlockSpec((g, q, rp), lambda s, o: (0, o[s], 0)),
                pl.BlockSpec((g, n, q), lambda s, o: (0, 0, o[s])),
                pl.BlockSpec((g, q, n), lambda s, o: (0, o[s], 0)),
                pl.BlockSpec((None, h2, q), lambda s, o: (o[s], 0, 0)),
                pl.BlockSpec((None, h2, q), lambda s, o: (o[s], 0, 0)),
                pl.BlockSpec((None, h2, q), lambda s, o: (o[s], 0, 0))]
    args = [xs, bt, cm, cumt, rdt, wt]
    scratch = [pltpu.VMEM((g, n, rp), F32)]
    if rev:
        yf, proj, za_blk, dexp, ng = extra
        in_specs += [pl.BlockSpec((g, q, rp), lambda s, o: (0, o[s], 0)),
                     pl.BlockSpec((q, d), lambda s, o: (o[s], za_blk)),
                     pl.BlockSpec((1, d), lambda s, o: (0, 0)),
                     pl.BlockSpec((1, d), lambda s, o: (0, 0))]
        args += [yf, proj, dexp, ng]
        out_shape = jax.ShapeDtypeStruct((mt, d), BF16)
        out_spec = pl.BlockSpec((q, d), lambda s, o: (o[s], 0))
        scratch.append(pltpu.VMEM((g, q, rp), F32))
    else:
        out_shape = jax.ShapeDtypeStruct((g, mt, rp), F32)
        out_spec = pl.BlockSpec((g, q, rp), lambda s, o: (0, o[s], 0))
    kern = functools.partial(_ssd_sweep_kernel, rev=rev, heads=heads, hpg=hpg, d=d)
    return pl.pallas_call(
        kern,
        out_shape=out_shape,
        grid_spec=pltpu.PrefetchScalarGridSpec(
            num_scalar_prefetch=1, grid=(nb,), in_specs=in_specs, out_specs=out_spec,
            scratch_shapes=scratch),
        compiler_params=_params("arbitrary"),
        name="ssd_sweep_bwd" if rev else "ssd_sweep_fwd",
    )(order, *args)


def _conformer_kernel(glu_ref, zb_ref, sh_ref, cw_ref, cb_ref, lg_ref, lb_ref, o_ref, pad_ref, v_ref,
                      *, seg, nseg, cf):
    taps = cw_ref.shape[0]
    half = taps // 2
    rc = min(seg, 64)
    ct = 2 * LANES
    rpad = seg + 2 * HALO
    u = glu_ref[:, 0:cf].astype(F32) * jax.nn.sigmoid(glu_ref[:, cf:2 * cf].astype(F32))
    for s in range(nseg):
        us = u[s * seg:(s + 1) * seg, :]
        hi = us.astype(BF16)
        lo = (us - hi.astype(F32)).astype(BF16)
        shifted = jnp.dot(sh_ref[...], jnp.concatenate([hi, lo], axis=0), preferred_element_type=F32)
        pad_ref[s] = shifted.reshape(SUBLANES, rpad, cf)
    for c0 in range(0, cf, ct):
        for s in range(nseg):
            for r0 in range(0, seg, rc):
                acc = jnp.broadcast_to(cb_ref[:, c0:c0 + ct], (rc, ct))
                for k in range(taps):
                    off = HALO + k - half
                    ph = off % SUBLANES
                    base = off - ph + r0
                    acc = acc + pad_ref[s, ph, base:base + rc, c0:c0 + ct] * cw_ref[k:k + 1, c0:c0 + ct]
                v_ref[s * seg + r0:s * seg + r0 + rc, c0:c0 + ct] = acc
    v = v_ref[...]
    mu = jnp.mean(v, axis=-1, keepdims=True)
    xc = v - mu
    y = xc * lax.rsqrt(jnp.mean(xc * xc, axis=-1, keepdims=True) + EPS) * lg_ref[...] + lb_ref[...]
    o_ref[...] = (_silu(y) * _silu(zb_ref[...].astype(F32))).astype(o_ref.dtype)


def _conformer(proj, conv_w, conv_b, ln_g, ln_b, *, row_off, rows, seg, glu_off, zb_off):
    taps, cf = conv_w.shape
    assert taps // 2 <= HALO
    tb = seg if seg >= 128 else 128
    nseg = tb // seg
    roff = row_off // tb
    rpad = seg + 2 * HALO
    r_idx = np.arange(rpad)[None, :, None]
    ph_idx = np.arange(SUBLANES)[:, None, None]
    t_idx = np.arange(seg)[None, None, :]
    shift = (r_idx == HALO - ph_idx + t_idx).astype(np.float32).reshape(SUBLANES * rpad, seg)
    shift2 = jnp.asarray(np.concatenate([shift, shift], axis=1)).astype(BF16)
    kern = functools.partial(_conformer_kernel, seg=seg, nseg=nseg, cf=cf)
    return pl.pallas_call(
        kern,
        out_shape=jax.ShapeDtypeStruct((rows, cf), BF16),
        grid=(rows // tb,),
        in_specs=[pl.BlockSpec((tb, 2 * cf), lambda i: (i + roff, glu_off // (2 * cf))),
                  pl.BlockSpec((tb, cf), lambda i: (i + roff, zb_off // cf)),
                  pl.BlockSpec(shift2.shape, lambda i: (0, 0)),
                  pl.BlockSpec((taps, cf), lambda i: (0, 0)),
                  pl.BlockSpec((1, cf), lambda i: (0, 0)),
                  pl.BlockSpec((1, cf), lambda i: (0, 0)),
                  pl.BlockSpec((1, cf), lambda i: (0, 0))],
        out_specs=pl.BlockSpec((tb, cf), lambda i: (i, 0)),
        scratch_shapes=[pltpu.VMEM((nseg, SUBLANES, rpad, cf), F32),
                        pltpu.VMEM((tb, cf), F32)],
        compiler_params=_params("parallel"),
        name="conformer",
    )(proj, proj, shift2, conv_w, conv_b.reshape(1, cf), ln_g.reshape(1, cf), ln_b.reshape(1, cf))


def _dft_cs(n):
    idx = np.arange(n)
    ang = 2.0 * np.pi * ((idx[:, None] * idx[None, :]) % n) / n
    return np.cos(ang), np.sin(ang)


def _fft_tables(l, gd):
    l1 = FFT_L1
    l2 = l // l1
    s = SUBLANES
    eye = np.eye(s)
    c1, s1 = _dft_cs(l1)
    c2, s2 = _dft_cs(l2)

    def kron_a(m):
        m4 = m.reshape(l1 // s, 1, s, l1, 1)
        return (m4 * eye.reshape(1, s, 1, 1, s)).reshape(l1 * s, l1 * s)

    def const_bf16(m):
        return jnp.asarray(m.astype(np.float32)).astype(BF16)

    ka = const_bf16(np.concatenate([kron_a(c1), kron_a(s1)], axis=0))
    mb = const_bf16(np.block([[c2, -s2], [s2, c2]]))
    k1 = np.arange(l1)[:, None]
    t2 = np.arange(l2)[None, :]
    ang = 2.0 * np.pi * ((k1 * t2) % l) / l

    def tw(m):
        m4 = m.astype(np.float32).reshape(l1 // s, s, l2, 1)
        return jnp.asarray(np.ascontiguousarray(np.broadcast_to(m4, (l1 // s, s, l2, LANES))))

    return ka, mb, tw(np.cos(ang)), tw(np.sin(ang)), _chan_table(l, gd)


def _chan_table(l, gd):
    cc, sc = _dft_cs(gd)
    return jnp.asarray(np.concatenate([cc, -sc], axis=0) / math.sqrt(l * gd), F32).astype(BF16)


def _fft_a_kernel(v_ref, ka_ref, or_ref, oi_ref):
    l1, rows, cb = v_ref.shape
    s = SUBLANES
    nkb = or_ref.shape[0]
    xf = v_ref[...].astype(F32)
    for h in range(rows // s):
        xb = xf[:, h * s:(h + 1) * s, :].reshape(l1 * s, cb).astype(BF16)
        y = jnp.dot(ka_ref[...], xb, preferred_element_type=F32)
        or_ref[:, h * s:(h + 1) * s, :, :] = y[0:l1 * s].reshape(nkb, s, s, cb)
        oi_ref[:, h * s:(h + 1) * s, :, :] = y[l1 * s:2 * l1 * s].reshape(nkb, s, s, cb)


def _fft_b_kernel(*refs, nh):
    ar_refs, ai_refs = refs[0:nh], refs[nh:2 * nh]
    twc_ref, tws_ref, mb_ref, tc_ref, o_ref = refs[2 * nh:2 * nh + 5]
    f_refs = refs[2 * nh + 5:]
    l2, s, _ = o_ref.shape
    for r in range(s):
        c, sn = twc_ref[r], tws_ref[r]
        br, bi = [], []
        for h in range(nh):
            ar = ar_refs[h][pl.ds(r, l2, stride=s), :]
            ai = ai_refs[h][pl.ds(r, l2, stride=s), :]
            br.append(ar * c - ai * sn)
            bi.append(ar * sn + ai * c)
        stack = jnp.concatenate([jnp.concatenate(br, axis=1), jnp.concatenate(bi, axis=1)], axis=0)
        z = jnp.dot(mb_ref[...], stack.astype(BF16), preferred_element_type=F32)
        xri = jnp.concatenate([z[0:l2], z[l2:2 * l2]], axis=1).astype(BF16)
        f = jnp.dot(xri, tc_ref[...], preferred_element_type=F32)
        for h in range(nh):
            f_refs[h][pl.ds(r, l2, stride=s), :] = f[:, h * LANES:(h + 1) * LANES]
    for h in range(nh):
        o_ref[:, :, h * LANES:(h + 1) * LANES] = f_refs[h][...].reshape(l2, s, LANES)


def _fourier_x(proj, tables, *, l, c, v_off):
    ka, mb, twc, tws, tc = tables
    l1, s = FFT_L1, SUBLANES
    l2 = l // l1
    gd = c // FN_GROUPS
    nkb = l1 // s
    mt, ncols = proj.shape
    tr = BF16_SUBLANES
    assert mt % l2 == 0 and l2 % tr == 0
    ap_shape = jax.ShapeDtypeStruct((nkb, l2, s, c), F32)
    ca = _pick_tile(math.gcd(c, v_off), 1024, LANES)
    voff = v_off // ca
    apr, api = pl.pallas_call(
        _fft_a_kernel,
        out_shape=(ap_shape, ap_shape),
        grid=(l2 // tr, c // ca),
        in_specs=[pl.BlockSpec((l1, tr, ca), lambda i, j: (0, i, voff + j)),
                  pl.BlockSpec(ka.shape, lambda i, j: (0, 0))],
        out_specs=(pl.BlockSpec((nkb, tr, s, ca), lambda i, j: (0, i, 0, j)),
                   pl.BlockSpec((nkb, tr, s, ca), lambda i, j: (0, i, 0, j))),
        compiler_params=_params("parallel", "parallel"),
        name="fft_stage_a",
    )(proj.reshape(mt // l2, l2, ncols), ka)
    nh = gd // LANES
    slab_specs = [pl.BlockSpec((None, l2 * s, LANES), lambda kq, j, h=h: (kq, 0, j * nh + h))
                  for h in range(nh)]
    apr2, api2 = apr.reshape(nkb, l2 * s, c), api.reshape(nkb, l2 * s, c)
    f = pl.pallas_call(
        functools.partial(_fft_b_kernel, nh=nh),
        out_shape=jax.ShapeDtypeStruct((l2, nkb, s, c), F32),
        grid=(nkb, c // gd),
        in_specs=slab_specs + slab_specs + [
            pl.BlockSpec((None, s, l2, LANES), lambda kq, j: (kq, 0, 0, 0)),
            pl.BlockSpec((None, s, l2, LANES), lambda kq, j: (kq, 0, 0, 0)),
            pl.BlockSpec(mb.shape, lambda kq, j: (0, 0)),
            pl.BlockSpec(tc.shape, lambda kq, j: (0, 0))],
        out_specs=pl.BlockSpec((l2, None, s, gd), lambda kq, j: (0, kq, 0, j)),
        scratch_shapes=[pltpu.VMEM((l2 * s, LANES), F32)] * nh,
        compiler_params=_params("parallel", "parallel"),
        name="fft_stage_b",
    )(*([apr2] * nh + [api2] * nh), twc, tws, mb, tc)
    return f.reshape(l, c)


def _gate_mul_kernel(f_ref, z_ref, o_ref):
    o_ref[...] = (f_ref[...] * _silu(z_ref[...].astype(F32))).astype(o_ref.dtype)


def _gate_mul(f, proj, zc_off):
    l, c = f.shape
    tm = _pick_tile(l, 512, BF16_SUBLANES)
    return pl.pallas_call(
        _gate_mul_kernel,
        out_shape=jax.ShapeDtypeStruct((l, c), BF16),
        grid=(l // tm,),
        in_specs=[pl.BlockSpec((tm, c), lambda i: (i, 0)),
                  pl.BlockSpec((tm, c), lambda i: (i, zc_off // c))],
        out_specs=pl.BlockSpec((tm, c), lambda i: (i, 0)),
        compiler_params=_params("parallel"),
        name="fourier_gate",
    )(f, proj)


def _fft_ctx_kernel(v_ref, z_ref, cs_ref, tc_ref, o_ref):
    lc = v_ref.shape[0]
    p = jnp.dot(cs_ref[...], v_ref[...], preferred_element_type=F32)
    pri = jnp.concatenate([p[0:lc], p[lc:2 * lc]], axis=1).astype(BF16)
    f = jnp.dot(pri, tc_ref[...], preferred_element_type=F32)
    o_ref[...] = (f * _silu(z_ref[...].astype(F32))).astype(o_ref.dtype)


def _fourier_ctx(proj, *, row_off, lc, c, v_off, zc_off):
    gd = c // FN_GROUPS
    cc, sc = _dft_cs(lc)
    cs = jnp.asarray(np.concatenate([cc, sc], axis=0), F32).astype(BF16)
    tc = _chan_table(lc, gd)
    rblk = row_off // lc
    return pl.pallas_call(
        _fft_ctx_kernel,
        out_shape=jax.ShapeDtypeStruct((lc, c), BF16),
        grid=(FN_GROUPS,),
        in_specs=[pl.BlockSpec((lc, gd), lambda g: (rblk, v_off // gd + g)),
                  pl.BlockSpec((lc, gd), lambda g: (rblk, zc_off // gd + g)),
                  pl.BlockSpec(cs.shape, lambda g: (0, 0)),
                  pl.BlockSpec(tc.shape, lambda g: (0, 0))],
        out_specs=pl.BlockSpec((lc, gd), lambda g: (0, g)),
        compiler_params=_params("parallel"),
        name="fourier_ctx",
    )(proj, proj, cs, tc)


def _merge_kernel(ya_ref, yb_ref, yc_ref, ga_ref, gb_ref, gc_ref, wa_ref, wb_ref, wc_ref, o_ref):
    def term(y_ref, w_ref, g_ref):
        return jax.nn.sigmoid(g_ref[...].astype(F32)) * jnp.dot(
            y_ref[...], w_ref[...], preferred_element_type=F32)
    acc = term(ya_ref, wa_ref, ga_ref) + term(yb_ref, wb_ref, gb_ref) + term(yc_ref, wc_ref, gc_ref)
    o_ref[...] = acc.astype(o_ref.dtype)


def _merge(ya, yb, yc, proj, w_branch, layer, *, rows, ya_row_off, proj_row_off, gate_off):
    d = ya.shape[1]
    cf, fw = yb.shape[1], yc.shape[1]
    tm = _pick_tile(rows, 1024, LANES)
    tn = _pick_tile(d, 256, LANES)
    ra, rpj = ya_row_off // tm, proj_row_off // tm
    gblk = gate_off // tn
    dblk = d // tn
    return pl.pallas_call(
        _merge_kernel,
        out_shape=jax.ShapeDtypeStruct((rows, d), BF16),
        grid=(rows // tm, d // tn),
        in_specs=[pl.BlockSpec((tm, d), lambda i, j: (i + ra, 0)),
                  pl.BlockSpec((tm, cf), lambda i, j: (i, 0)),
                  pl.BlockSpec((tm, fw), lambda i, j: (i, 0)),
                  pl.BlockSpec((tm, tn), lambda i, j: (i + rpj, gblk + j)),
                  pl.BlockSpec((tm, tn), lambda i, j: (i + rpj, gblk + dblk + j)),
                  pl.BlockSpec((tm, tn), lambda i, j: (i + rpj, gblk + 2 * dblk + j)),
                  pl.BlockSpec((None, d, tn), lambda i, j: (layer, 0, j)),
                  pl.BlockSpec((None, cf, tn), lambda i, j: (layer, d // cf, j)),
                  pl.BlockSpec((None, fw, tn), lambda i, j: (layer, (d + cf) // fw, j))],
        out_specs=pl.BlockSpec((tm, tn), lambda i, j: (i, j)),
        compiler_params=_params("parallel", "parallel"),
        name="merge",
    )(ya, yb, yc, proj, proj, proj, w_branch, w_branch, w_branch)


def _column_layout(d, cf, fw, heads):
    gn = SSD_GROUPS * SSD_STATE
    cc = d + 2 * gn
    src = {}
    pos = 0
    for name, width in (("xbc", cc), ("za", d), ("dt", 2 * heads), ("glu", 2 * cf), ("zb", cf),
                        ("v", fw), ("zc", fw), ("gates", 3 * d)):
        src[name] = (pos, width)
        pos += width
    order = (("xbc", cc), ("zb", cf), ("v", fw), ("zc", fw), ("za", d), ("glu", 2 * cf),
             ("gates", d), ("dt", 2 * heads))
    dst = {}
    pieces = []
    pos = 0
    for name, align in order:
        padw = (-pos) % align
        if padw:
            pieces.append(("pad", padw))
            pos += padw
        dst[name] = pos
        pieces.append((name, src[name]))
        pos += src[name][1]
    return dst, pieces, pos


def _source_columns(pieces):
    main = [(name, spec) for name, spec in pieces if name != "dt"]
    widths = [spec if name == "pad" else spec[1] for name, spec in main]
    tn = _pick_tile(functools.reduce(math.gcd, widths), 512, LANES)
    cols = []
    for name, spec in main:
        start, width = (0, spec) if name == "pad" else spec
        cols += [start + k * tn if name != "pad" else 0 for k in range(width // tn)]
    return tn, cols


def kernel(x, c, ctx, c_ctx, w_mod, b_mod, norm_g, w_in, ssd_conv_w, ssd_conv_b, ssd_dt_bias,
           ssd_a_log, ssd_d, ssd_norm_g, cf_conv_w, cf_conv_b, cf_ln_g, cf_ln_b, w_branch,
           w_out, final_g):
    bsz, l, d = x.shape
    assert bsz == 1, "single-sample kernel"
    lc = ctx.shape[1]
    depth = w_mod.shape[0]
    heads = ssd_dt_bias.shape[-1]
    cf = cf_conv_w.shape[-1]
    fw = w_branch.shape[1] - d - cf
    p = d // heads
    assert 2 * p == LANES and l % SSD_CHUNK == 0 and lc % SSD_CHUNK == 0 and l % lc == 0
    mt = l + lc
    nbx, nb = l // SSD_CHUNK, mt // SSD_CHUNK

    dst, pieces, _ = _column_layout(d, cf, fw, heads)
    tn_in, src_cols = _source_columns(pieces)
    dt_src = [spec[0] for name, spec in pieces if name == "dt"]

    cond8 = jnp.zeros((SUBLANES, d), F32).at[0].set(c[0]).at[1].set(c_ctx)
    mod_all = _modulation(cond8, w_mod, b_mod)

    order_f = jnp.asarray(list(range(nbx, nb)) + list(range(nbx)), jnp.int32)
    order_b = jnp.asarray(list(range(nb - 1, nbx - 1, -1)) + list(range(nbx - 1, -1, -1)), jnp.int32)
    tables = _fft_tables(l, fw // FN_GROUPS)

    xs2 = x[0]
    cs2 = ctx[0]
    w_br = w_branch.astype(BF16)
    for i in range(depth):
        update_ctx = i < depth - 1
        hcat = _norm_mod(xs2, cs2, norm_g[i], mod_all, i)
        proj = _in_proj(hcat, w_in, i, src_cols, tn_in, BF16, "in_proj")
        dt_raw = _in_proj(hcat, w_in, i, dt_src, 2 * heads, F32, "in_proj_dt")

        xs_g, bt_g, c_g, cumt, rdt, wt = _ssd_pre(
            proj, dt_raw, ssd_conv_w[i], ssd_conv_b[i], ssd_dt_bias[i].reshape(-1),
            ssd_a_log[i].reshape(-1), nx=l, d=d, heads=heads)
        y_f = _ssd_sweep(order_f, xs_g, bt_g, c_g, cumt, rdt, wt, rev=False, heads=heads, d=d)
        dexp = jnp.repeat(ssd_d[i], p).reshape(1, d)
        ya = _ssd_sweep(order_b, xs_g, bt_g, c_g, cumt, rdt, wt, rev=True, heads=heads, d=d,
                        extra=(y_f, proj, dst["za"] // d, dexp, ssd_norm_g[i].reshape(1, d)))

        yb = _conformer(proj, cf_conv_w[i], cf_conv_b[i], cf_ln_g[i], cf_ln_b[i],
                        row_off=0, rows=l, seg=GRID_W, glu_off=dst["glu"], zb_off=dst["zb"])
        yc = _gate_mul(_fourier_x(proj, tables, l=l, c=fw, v_off=dst["v"]), proj, dst["zc"])
        merged = _merge(ya, yb, yc, proj, w_br, i, rows=l, ya_row_off=0, proj_row_off=0,
                        gate_off=dst["gates"])
        xs2 = _out_proj(merged, w_out, xs2, mod_all, i, 0)
        if update_ctx:
            yb_c = _conformer(proj, cf_conv_w[i], cf_conv_b[i], cf_ln_g[i], cf_ln_b[i],
                              row_off=l, rows=lc, seg=lc, glu_off=dst["glu"], zb_off=dst["zb"])
            yc_c = _fourier_ctx(proj, row_off=l, lc=lc, c=fw, v_off=dst["v"], zc_off=dst["zc"])
            merged_c = _merge(ya, yb_c, yc_c, proj, w_br, i, rows=lc, ya_row_off=l, proj_row_off=l,
                              gate_off=dst["gates"])
            cs2 = _out_proj(merged_c, w_out, cs2, mod_all, i, 1)
    return _final_norm(xs2, final_g)[None]
```
